```python
import math
import jax
import jax.numpy as jnp
from jax import lax
import numpy as np

D_MODEL = 1024
BATCH = 8
SEQ = 8192
DEPTH = 2

CHUNK = 64
PLE_DIM = 256
N_BRANCH = 4
LN_EPS = 1e-5
DEEPNORM_ALPHA = (2 * DEPTH) ** 0.25
DEEPNORM_BETA = (8 * DEPTH) ** -0.25

SB_HEADS = 4
SB_HEAD_DIM = 64
SB_BLOCK = 128
SB_W = SB_HEADS * SB_HEAD_DIM
SG_GROUPS = 4
SG_GROUP_DIM = 64
SG_CHUNK = 128
SG_W = SG_GROUPS * SG_GROUP_DIM
CA_HEADS = 4
CA_HEAD_DIM = 64
CA_LEFT_CHUNKS = 8
CA_BAND = (CA_LEFT_CHUNKS + 1) * CHUNK
CA_REL_MAX = 256
CA_REL_SIZE = (CHUNK - 1) + CA_REL_MAX + 1
CA_W = CA_HEADS * CA_HEAD_DIM
SSD_HEADS = 8
SSD_HEAD_DIM = 64
SSD_GROUPS = 2
SSD_HPG = SSD_HEADS // SSD_GROUPS
SSD_STATE = 64
SSD_CONV = 4
SSD_CHUNK = CHUNK
SSD_INNER = SSD_HEADS * SSD_HEAD_DIM
SSD_CONV_DIM = SSD_INNER + 2 * SSD_GROUPS * SSD_STATE
D_FF = 2816
N_EXPERTS = 8
TOP_K = 2
D_FF_EXPERT = 3584
N_DENSE = (DEPTH + 1) // 2
N_MOE = DEPTH // 2

COLS_A = 3 * SB_W
COLS_B = 2 * SG_W
COLS_C = 3 * CA_W
COLS_D = SSD_INNER + SSD_CONV_DIM + SSD_HEADS
COLS_GATE = N_BRANCH * D_MODEL
D_IN = COLS_A + COLS_B + COLS_C + COLS_D + COLS_GATE

kernel_name = 'hybrid_sb_sgmlp_band_ssd_moe_trunk'


def _split_last(t, sizes):
    out, start = [], 0
    for n in sizes:
        out.append(t[..., start:start + n])
        start += n
    return out


def layer_norm(x, g, b):
    xf = x.astype(jnp.float32)
    mu = jnp.mean(xf, axis=-1, keepdims=True)
    var = jnp.mean(jnp.square(xf - mu), axis=-1, keepdims=True)
    return ((xf - mu) * lax.rsqrt(var + LN_EPS) * g + b).astype(x.dtype)


def stick_breaking_attention(q, k, v):
    bsz, seq = q.shape[0], q.shape[1]
    nb = seq // SB_BLOCK
    scale = SB_HEAD_DIM ** -0.5
    kf = k.astype(jnp.float32)
    vf = v.astype(jnp.float32)
    key_pos = jnp.arange(seq)
    q_blocks = jnp.moveaxis(q.reshape(bsz, nb, SB_BLOCK, SB_HEADS, SB_HEAD_DIM), 1, 0)

    def block(args):
        qb, bi = args
        z = jnp.einsum('bthd,bshd->bhts', qb.astype(jnp.float32), kf) * scale
        q_pos = bi * SB_BLOCK + jnp.arange(SB_BLOCK)
        mask = key_pos[None, :] < q_pos[:, None]
        log_beta = jnp.where(mask, jax.nn.log_sigmoid(z), -jnp.inf)
        log_keep = jnp.where(mask, jax.nn.log_sigmoid(-z), 0.0)
        later = lax.cumsum(log_keep, axis=3, reverse=True) - log_keep
        a = jnp.exp(log_beta + later)
        return jnp.einsum('bhts,bshd->bthd', a, vf)

    out = lax.map(block, (q_blocks, jnp.arange(nb)))
    return jnp.moveaxis(out, 0, 1).reshape(bsz, seq, SB_W).astype(q.dtype)


def spatial_gating(uv, ln_g, ln_b, w_s, b_s):
    bsz, seq = uv.shape[0], uv.shape[1]
    u, v = uv[..., :SG_W], uv[..., SG_W:]
    v = layer_norm(v, ln_g, ln_b)
    nc = seq // SG_CHUNK
    v = v.reshape(bsz, nc, SG_CHUNK, SG_GROUPS, SG_GROUP_DIM)
    causal = jnp.tril(jnp.ones((SG_CHUNK, SG_CHUNK), dtype=bool))
    w = jnp.where(causal[None], w_s, jnp.zeros_like(w_s))
    mixed = jnp.einsum('gts,bcsgd->bctgd', w, v) + jnp.transpose(b_s)[None, None, :, :, None]
    return u * mixed.reshape(bsz, seq, SG_W)


def chunk_band_attention(q, k, v, rel_bias):
    bsz, seq = q.shape[0], q.shape[1]
    nc = seq // CHUNK
    scale = CA_HEAD_DIM ** -0.5

    def chunks(t):
        return t.reshape(bsz, nc, CHUNK, CA_HEADS, CA_HEAD_DIM)

    pad = ((0, 0), (CA_LEFT_CHUNKS, 0), (0, 0), (0, 0), (0, 0))
    kp = jnp.pad(chunks(k), pad)
    vp = jnp.pad(chunks(v), pad)
    band_idx = jnp.arange(nc)[:, None] + jnp.arange(CA_LEFT_CHUNKS + 1)[None, :]
    kb = kp[:, band_idx].reshape(bsz, nc, CA_BAND, CA_HEADS, CA_HEAD_DIM)
    vb = vp[:, band_idx].reshape(bsz, nc, CA_BAND, CA_HEADS, CA_HEAD_DIM)
    scores = jnp.einsum('bcihd,bcjhd->bchij', chunks(q).astype(jnp.float32),
                        kb.astype(jnp.float32)) * scale
    i = jnp.arange(CHUNK)[:, None]
    j = jnp.arange(CA_BAND)[None, :]
    rel = CA_LEFT_CHUNKS * CHUNK + i - j
    rel_idx = jnp.clip(rel, -(CHUNK - 1), CA_REL_MAX) + (CHUNK - 1)
    bias = rel_bias.astype(jnp.float32)[:, rel_idx]
    key_pos = (jnp.arange(nc)[:, None] - CA_LEFT_CHUNKS) * CHUNK + jnp.arange(CA_BAND)[None, :]
    valid = key_pos >= 0
    scores = jnp.where(valid[None, :, None, None, :], scores + bias[None, None], -jnp.inf)
    probs = jax.nn.softmax(scores, axis=-1)
    out = jnp.einsum('bchij,bcjhd->bcihd', probs, vb.astype(jnp.float32))
    return out.reshape(bsz, seq, CA_W).astype(q.dtype)


def causal_depthwise_conv(x, w):
    return lax.conv_general_dilated(
        x, w[:, None, :], window_strides=(1,), padding=[(SSD_CONV - 1, 0)],
        dimension_numbers=('NWC', 'WIO', 'NWC'), feature_group_count=x.shape[-1])


def ssd_mixer(zxbcdt, conv_w, conv_b, dt_bias, a_log, d_skip, norm_g):
    bsz, seq = zxbcdt.shape[0], zxbcdt.shape[1]
    f32 = jnp.float32
    z, xbc, dt = _split_last(zxbcdt, [SSD_INNER, SSD_CONV_DIM, SSD_HEADS])
    xbc = jax.nn.silu(causal_depthwise_conv(xbc, conv_w) + conv_b)
    xs, bm, cm = _split_last(xbc, [SSD_INNER, SSD_GROUPS * SSD_STATE, SSD_GROUPS * SSD_STATE])
    dt = jax.nn.softplus(dt.astype(f32) + dt_bias)
    a = -jnp.exp(a_log.astype(f32))
    nc = seq // SSD_CHUNK
    x = xs.astype(f32).reshape(bsz, nc, SSD_CHUNK, SSD_GROUPS, SSD_HPG, SSD_HEAD_DIM)
    bm = bm.astype(f32).reshape(bsz, nc, SSD_CHUNK, SSD_GROUPS, SSD_STATE)
    cm = cm.astype(f32).reshape(bsz, nc, SSD_CHUNK, SSD_GROUPS, SSD_STATE)
    dt = dt.reshape(bsz, nc, SSD_CHUNK, SSD_GROUPS, SSD_HPG)
    a_cs = jnp.cumsum(dt * a.reshape(SSD_GROUPS, SSD_HPG), axis=2)
    xdt = x * dt[..., None]
    causal = jnp.tril(jnp.ones((SSD_CHUNK, SSD_CHUNK), dtype=bool))[None, None, :, :, None, None]
    seg = a_cs[:, :, :, None] - a_cs[:, :, None, :]
    decay = jnp.exp(jnp.where(causal, seg, -jnp.inf))
    cb = jnp.einsum('bclgn,bcsgn->bclsg', cm, bm)
    y_diag = jnp.einsum('bclsgh,bcsghp->bclghp', cb[..., None] * decay, xdt)
    decay_to_end = jnp.exp(a_cs[:, :, -1:] - a_cs)
    states = jnp.einsum('bcsgn,bcsgh,bcsghp->bcghpn', bm, decay_to_end, xdt)
    chunk_decay = jnp.exp(a_cs[:, :, -1])

    def step(h, inp):
        st, dec = inp
        return h * dec[..., None, None] + st, h

    h0 = jnp.zeros((bsz, SSD_GROUPS, SSD_HPG, SSD_HEAD_DIM, SSD_STATE), f32)
    _, h_prev = lax.scan(step, h0, (jnp.moveaxis(states, 1, 0), jnp.moveaxis(chunk_decay, 1, 0)))
    h_prev = jnp.moveaxis(h_prev, 0, 1)
    y_off = jnp.einsum('bclgn,bcghpn,bclgh->bclghp', cm, h_prev, jnp.exp(a_cs))
    y = y_diag + y_off + x * d_skip.astype(f32).reshape(SSD_GROUPS, SSD_HPG)[..., None]
    y = y.reshape(bsz, seq, SSD_INNER) * jax.nn.silu(z.astype(f32))
    yg = y.reshape(bsz, seq, SSD_GROUPS, SSD_INNER // SSD_GROUPS)
    yg = yg * lax.rsqrt(jnp.mean(jnp.square(yg), axis=-1, keepdims=True) + LN_EPS)
    return (yg.reshape(bsz, seq, SSD_INNER) * norm_g).astype(zxbcdt.dtype)


def swiglu(x, w_gate, w_up, w_down):
    return (jax.nn.silu(x @ w_gate) * (x @ w_up)) @ w_down


def moe_swiglu(x, w_router, w_gate, w_up, w_down):
    logits = jnp.einsum('bsd,de->bse', x, w_router).astype(jnp.float32)
    top_val, top_idx = lax.top_k(logits, TOP_K)
    gates = jax.nn.softmax(top_val, axis=-1)
    combine = jnp.sum(jax.nn.one_hot(top_idx, N_EXPERTS, dtype=jnp.float32) * gates[..., None],
                      axis=-2).astype(x.dtype)
    y = jnp.zeros_like(x)
    for e in range(N_EXPERTS):
        y = y + combine[..., e:e + 1] * swiglu(x, w_gate[e], w_up[e], w_down[e])
    return y


def setup_inputs(seed: int = 0) -> dict:
    key = jax.random.key(seed)
    k = jax.random.split(key, 32)
    f32 = jnp.float32

    def nrm(i, shape, scale):
        return jax.random.normal(k[i], shape, f32) * scale

    dt0 = jnp.exp(jax.random.uniform(k[15], (DEPTH, SSD_HEADS), f32,
                                     math.log(1e-3), math.log(1e-1)))
    return {
        'x': nrm(0, (BATCH, SEQ, D_MODEL), 1.0),
        'p': nrm(1, (DEPTH, BATCH, SEQ, PLE_DIM), 1.0),
        'w_in': nrm(2, (DEPTH, D_MODEL, D_IN), D_MODEL ** -0.5),
        'w_br_a': nrm(3, (DEPTH, SB_W, D_MODEL), SB_W ** -0.5),
        'w_br_b': nrm(4, (DEPTH, SG_W, D_MODEL), SG_W ** -0.5),
        'w_br_c': nrm(5, (DEPTH, CA_W, D_MODEL), CA_W ** -0.5),
        'w_br_d': nrm(6, (DEPTH, SSD_INNER, D_MODEL), SSD_INNER ** -0.5),
        'w_out': nrm(7, (DEPTH, D_MODEL, D_MODEL), DEEPNORM_BETA * D_MODEL ** -0.5),
        'sg_ln_g': 1.0 + nrm(8, (DEPTH, SG_W), 0.02),
        'sg_ln_b': nrm(9, (DEPTH, SG_W), 0.02),
        'sg_w': nrm(10, (DEPTH, SG_GROUPS, SG_CHUNK, SG_CHUNK), SG_CHUNK ** -0.5),
        'sg_b': 1.0 + nrm(11, (DEPTH, SG_GROUPS, SG_CHUNK), 0.1),
        'ca_rel_bias': nrm(12, (DEPTH, CA_HEADS, CA_REL_SIZE), 0.1),
        'ssd_conv_w': nrm(13, (DEPTH, SSD_CONV, SSD_CONV_DIM), SSD_CONV ** -0.5),
        'ssd_conv_b': nrm(14, (DEPTH, SSD_CONV_DIM), 0.02),
        'ssd_dt_bias': dt0 + jnp.log(-jnp.expm1(-dt0)),
        'ssd_a_log': jnp.log(jax.random.uniform(k[16], (DEPTH, SSD_HEADS), f32, 1.0, 16.0)),
        'ssd_d': 1.0 + nrm(17, (DEPTH, SSD_HEADS), 0.1),
        'ssd_norm_g': 1.0 + nrm(18, (DEPTH, SSD_INNER), 0.02),
        'ln1_g': 1.0 + nrm(19, (DEPTH, D_MODEL), 0.02),
        'ln1_b': nrm(20, (DEPTH, D_MODEL), 0.02),
        'ffn_w_gate': nrm(21, (N_DENSE, D_MODEL, D_FF), D_MODEL ** -0.5),
        'ffn_w_up': nrm(22, (N_DENSE, D_MODEL, D_FF), D_MODEL ** -0.5),
        'ffn_w_down': nrm(23, (N_DENSE, D_FF, D_MODEL), DEEPNORM_BETA * D_FF ** -0.5),
        'moe_router': nrm(24, (N_MOE, D_MODEL, N_EXPERTS), D_MODEL ** -0.5),
        'moe_w_gate': nrm(25, (N_MOE, N_EXPERTS, D_MODEL, D_FF_EXPERT), D_MODEL ** -0.5),
        'moe_w_up': nrm(26, (N_MOE, N_EXPERTS, D_MODEL, D_FF_EXPERT), D_MODEL ** -0.5),
        'moe_w_down': nrm(27, (N_MOE, N_EXPERTS, D_FF_EXPERT, D_MODEL),
                          DEEPNORM_BETA * D_FF_EXPERT ** -0.5),
        'ple_w_gate': nrm(28, (DEPTH, D_MODEL, D_MODEL), D_MODEL ** -0.5),
        'ple_w_proj': nrm(29, (DEPTH, PLE_DIM, D_MODEL), DEEPNORM_BETA * PLE_DIM ** -0.5),
        'ln2_g': 1.0 + nrm(30, (DEPTH, D_MODEL), 0.02),
        'ln2_b': nrm(31, (DEPTH, D_MODEL), 0.02),
    }


def reference(x, p, w_in, w_br_a, w_br_b, w_br_c, w_br_d, w_out, sg_ln_g, sg_ln_b, sg_w, sg_b,
              ca_rel_bias, ssd_conv_w, ssd_conv_b, ssd_dt_bias, ssd_a_log, ssd_d, ssd_norm_g,
              ln1_g, ln1_b, ffn_w_gate, ffn_w_up, ffn_w_down, moe_router, moe_w_gate, moe_w_up,
              moe_w_down, ple_w_gate, ple_w_proj, ln2_g, ln2_b):
    bsz, seq = x.shape[0], x.shape[1]
    for i in range(DEPTH):
        h = x @ w_in[i]
        qkv_a, uv_b, qkv_c, zxbcdt_d, gate_logits = _split_last(
            h, [COLS_A, COLS_B, COLS_C, COLS_D, COLS_GATE])
        qkv_a = qkv_a.reshape(bsz, seq, 3, SB_HEADS, SB_HEAD_DIM)
        y_a = stick_breaking_attention(qkv_a[:, :, 0], qkv_a[:, :, 1], qkv_a[:, :, 2])
        y_b = spatial_gating(jax.nn.gelu(uv_b), sg_ln_g[i], sg_ln_b[i], sg_w[i], sg_b[i])
        qkv_c = qkv_c.reshape(bsz, seq, 3, CA_HEADS, CA_HEAD_DIM)
        y_c = chunk_band_attention(qkv_c[:, :, 0], qkv_c[:, :, 1], qkv_c[:, :, 2], ca_rel_bias[i])
        y_d = ssd_mixer(zxbcdt_d, ssd_conv_w[i], ssd_conv_b[i], ssd_dt_bias[i], ssd_a_log[i],
                        ssd_d[i], ssd_norm_g[i])
        g = jax.nn.sigmoid(gate_logits.reshape(bsz, seq, N_BRANCH, D_MODEL))
        merged = (g[:, :, 0] * (y_a @ w_br_a[i]) + g[:, :, 1] * (y_b @ w_br_b[i])
                  + g[:, :, 2] * (y_c @ w_br_c[i]) + g[:, :, 3] * (y_d @ w_br_d[i]))
        x = layer_norm(DEEPNORM_ALPHA * x + merged @ w_out[i], ln1_g[i], ln1_b[i])
        if i % 2 == 0:
            f = swiglu(x, ffn_w_gate[i // 2], ffn_w_up[i // 2], ffn_w_down[i // 2])
        else:
            f = moe_swiglu(x, moe_router[i // 2], moe_w_gate[i // 2], moe_w_up[i // 2],
                           moe_w_down[i // 2])
        ple = jax.nn.sigmoid(x @ ple_w_gate[i]) * (p[i] @ ple_w_proj[i])
        x = layer_norm(DEEPNORM_ALPHA * x + f + ple, ln2_g[i], ln2_b[i])
    return x
```

```python
import functools

import jax
import jax.numpy as jnp
from jax import lax
from jax.experimental import pallas as pl
from jax.experimental.pallas import tpu as pltpu

F32 = jnp.float32
BF16 = jnp.bfloat16

D_MODEL = 1024
DEPTH = 2
LN_EPS = 1e-5
DEEPNORM_ALPHA = (2 * DEPTH) ** 0.25
HEAD_DIM = 64
MIX_W = 256
SG_CHUNK = 128
BAND_CHUNK = 64
BAND_LEFT = 8
BAND_REL_MAX = 256
SSD_INNER = 512
SSD_HEADS = 8
SSD_BC = 256
N_EXPERTS = 8
NEG_BIG = -1e30

COL_GATE = 0
COL_A = 4096
COL_B = 4864
COL_C = 5376
COL_Z = 6144
COL_XS = 6656
COL_BC = 7168
COL_DT = 7424
IN_COLS = 7680

VMEM_LIMIT = 56 * 1024 * 1024


def _cparams(*sem):
    return pltpu.CompilerParams(dimension_semantics=sem, vmem_limit_bytes=VMEM_LIMIT)


def _nt_dot(a, b):
    return lax.dot_general(a, b, (((1,), (1,)), ((), ())), preferred_element_type=F32)


def _dot(a, b):
    return jnp.dot(a, b, preferred_element_type=F32)


def _softplus(x):
    return jnp.maximum(x, 0.0) + jnp.log(1.0 + jnp.exp(-jnp.abs(x)))


def _silu(x):
    return x * jax.nn.sigmoid(x)


def _split2(x):
    hi = x.astype(BF16)
    lo = (x - hi.astype(F32)).astype(BF16)
    return hi, lo


def _split3(x):
    hi = x.astype(BF16)
    r = x - hi.astype(F32)
    mid = r.astype(BF16)
    lo = (r - mid.astype(F32)).astype(BF16)
    return hi, mid, lo


def _layer_norm(x, g, b):
    mu = jnp.mean(x, axis=-1, keepdims=True)
    xc = x - mu
    var = jnp.mean(xc * xc, axis=-1, keepdims=True)
    return xc * lax.rsqrt(var + LN_EPS) * g + b


def _inproj_kernel(x_ref, w_ref, o_ref):
    o_ref[...] = _dot(x_ref[...].astype(BF16), w_ref[...]).astype(o_ref.dtype)


def _inproj(x2, w, tm, tn):
    n, d = x2.shape
    nc = w.shape[1]
    return pl.pallas_call(
        _inproj_kernel,
        grid=(n // tm, nc // tn),
        in_specs=[pl.BlockSpec((tm, d), lambda i, j: (i, 0)),
                  pl.BlockSpec((d, tn), lambda i, j: (0, j))],
        out_specs=pl.BlockSpec((tm, tn), lambda i, j: (i, j)),
        out_shape=jax.ShapeDtypeStruct((n, nc), BF16),
        compiler_params=_cparams("parallel", "arbitrary"),
        name="inproj",
    )(x2, w)


def _sb_kernel(q_ref, k_ref, v_ref, o_ref, acc_ref, run_ref, *, blk):
    qi = pl.program_id(1)
    lane = lax.broadcasted_iota(jnp.int32, (1, MIX_W), 1)
    head_mask = [(lane // HEAD_DIM) == h for h in range(4)]
    q = q_ref[...]
    q_heads = [jnp.where(head_mask[h], q, jnp.zeros_like(q)) for h in range(4)]
    row = lax.broadcasted_iota(jnp.int32, (blk, blk), 0)
    col = lax.broadcasted_iota(jnp.int32, (blk, blk), 1)
    below_diag = col < row
    suffix = jnp.where(row >= col, 1.0, 0.0).astype(BF16)

    acc_ref[...] = jnp.zeros_like(acc_ref)
    run_ref[...] = jnp.zeros_like(run_ref)

    def body(j, carry):
        kb = qi - j
        start = pl.multiple_of(kb * blk, blk)
        k_blk = k_ref[pl.ds(start, blk), :]
        v_blk = v_ref[pl.ds(start, blk), :]
        valid = jnp.logical_or(below_diag, j > 0)
        weights = []
        for h in range(4):
            z = _nt_dot(q_heads[h], k_blk)
            sp = jnp.where(valid, _softplus(z), 0.0)
            hi, lo = _split2(sp)
            cs = _dot(hi, suffix) + _dot(lo, suffix)
            run = run_ref[h]
            w = jnp.exp(jnp.where(valid, z - cs - run, NEG_BIG))
            run_ref[h] = run + cs[:, 0:1]
            weights.append(w.astype(BF16))
        wcat = jnp.concatenate(weights, axis=1)
        vcat = jnp.concatenate(
            [jnp.where(head_mask[h], v_blk, jnp.zeros_like(v_blk)) for h in range(4)], axis=0)
        acc_ref[...] += _dot(wcat, vcat)
        return carry

    lax.fori_loop(0, qi + 1, body, 0)
    o_ref[...] = acc_ref[...].astype(o_ref.dtype)


def _stick_breaking(h, bsz, seq, blk):
    n = bsz * seq
    nq = seq // blk
    cq, ck, cv = (COL_A // MIX_W + i for i in range(3))
    return pl.pallas_call(
        functools.partial(_sb_kernel, blk=blk),
        grid=(bsz, nq),
        in_specs=[pl.BlockSpec((blk, MIX_W), lambda b, i: (b * nq + i, cq)),
                  pl.BlockSpec((seq, MIX_W), lambda b, i: (b, ck)),
                  pl.BlockSpec((seq, MIX_W), lambda b, i: (b, cv))],
        out_specs=pl.BlockSpec((blk, MIX_W), lambda b, i: (b * nq + i, 0)),
        out_shape=jax.ShapeDtypeStruct((n, MIX_W), BF16),
        scratch_shapes=[pltpu.VMEM((blk, MIX_W), F32), pltpu.VMEM((4, blk, 1), F32)],
        compiler_params=_cparams("parallel", "arbitrary"),
        name="stick_breaking",
    )(h, h, h)


def _gelu_tanh(x):
    return 0.5 * x * (1.0 + jnp.tanh(0.7978845608028654 * (x + 0.044715 * (x * x * x))))


def _sg_kernel(u_ref, v_ref, g_ref, b_ref, w_ref, bias_ref, o_ref, *, tb):
    u = _gelu_tanh(u_ref[...].astype(F32))
    v = _gelu_tanh(v_ref[...].astype(F32))
    vn = _layer_norm(v, g_ref[...], b_ref[...]).astype(BF16)
    lane = lax.broadcasted_iota(jnp.int32, (1, MIX_W), 1)
    group_mask = [(lane // HEAD_DIM) == g for g in range(4)]
    row = lax.broadcasted_iota(jnp.int32, (SG_CHUNK, SG_CHUNK), 0)
    col = lax.broadcasted_iota(jnp.int32, (SG_CHUNK, SG_CHUNK), 1)
    causal = col <= row
    wcat = jnp.concatenate(
        [jnp.where(causal, w_ref[g], 0.0).astype(BF16) for g in range(4)], axis=1)
    bias = bias_ref[...]
    for c in range(tb // SG_CHUNK):
        sl = slice(c * SG_CHUNK, (c + 1) * SG_CHUNK)
        vc = vn[sl, :]
        vstack = jnp.concatenate(
            [jnp.where(group_mask[g], vc, jnp.zeros_like(vc)) for g in range(4)], axis=0)
        mixed = _dot(wcat, vstack) + bias
        o_ref[sl, :] = (u[sl, :] * mixed).astype(o_ref.dtype)


def _spatial_gating(h, ln_g, ln_b, w_s, bias_full, tb):
    n = h.shape[0]
    cu, cv = COL_B // MIX_W, COL_B // MIX_W + 1
    return pl.pallas_call(
        functools.partial(_sg_kernel, tb=tb),
        grid=(n // tb,),
        in_specs=[pl.BlockSpec((tb, MIX_W), lambda i: (i, cu)),
                  pl.BlockSpec((tb, MIX_W), lambda i: (i, cv)),
                  pl.BlockSpec((1, MIX_W), lambda i: (0, 0)),
                  pl.BlockSpec((1, MIX_W), lambda i: (0, 0)),
                  pl.BlockSpec((4, SG_CHUNK, SG_CHUNK), lambda i: (0, 0, 0)),
                  pl.BlockSpec((SG_CHUNK, MIX_W), lambda i: (0, 0))],
        out_specs=pl.BlockSpec((tb, MIX_W), lambda i: (i, 0)),
        out_shape=jax.ShapeDtypeStruct((n, MIX_W), BF16),
        compiler_params=_cparams("parallel"),
        name="spatial_gating",
    )(h, h, ln_g, ln_b, w_s, bias_full)


BAND_TQ = 256
BAND_PREV = BAND_LEFT * BAND_CHUNK
BAND_WIN = BAND_PREV + BAND_TQ


def _band_kernel(q_ref, k2_ref, k1_ref, k0_ref, v2_ref, v1_ref, v0_ref, bias_ref, o_ref, *, nblk):
    bi = pl.program_id(0) % nblk
    lane = lax.broadcasted_iota(jnp.int32, (1, MIX_W), 1)
    q = q_ref[...]
    kcat = jnp.concatenate([k2_ref[...], k1_ref[...], k0_ref[...]], axis=0)
    vcat = jnp.concatenate([v2_ref[...], v1_ref[...], v0_ref[...]], axis=0)
    col = lax.broadcasted_iota(jnp.int32, (1, BAND_WIN), 1)
    in_seq = col >= (2 - jnp.minimum(bi, 2)) * BAND_TQ
    out = jnp.zeros((BAND_TQ, MIX_W), F32)
    for h in range(4):
        hm = (lane // HEAD_DIM) == h
        s = _nt_dot(jnp.where(hm, q, jnp.zeros_like(q)), kcat) + bias_ref[h]
        s = jnp.where(in_seq, s, NEG_BIG)
        m = jnp.max(s, axis=-1, keepdims=True)
        p = jnp.exp(s - m)
        l = jnp.sum(p, axis=-1, keepdims=True)
        o = _dot(p.astype(BF16), vcat) / l
        out = jnp.where(hm, o, out)
    o_ref[...] = out.astype(o_ref.dtype)


def _band_attention(h, bias_full, seq):
    n = h.shape[0]
    nblk = seq // BAND_TQ
    cq, ck, cv = (COL_C // MIX_W + i for i in range(3))

    def prev(i, d):
        return i - jnp.minimum(i % nblk, d)

    def spec(c, d):
        return pl.BlockSpec((BAND_TQ, MIX_W), lambda i: (prev(i, d), c))

    return pl.pallas_call(
        functools.partial(_band_kernel, nblk=nblk),
        grid=(n // BAND_TQ,),
        in_specs=[spec(cq, 0), spec(ck, 2), spec(ck, 1), spec(ck, 0),
                  spec(cv, 2), spec(cv, 1), spec(cv, 0),
                  pl.BlockSpec((4, BAND_TQ, BAND_WIN), lambda i: (0, 0, 0))],
        out_specs=pl.BlockSpec((BAND_TQ, MIX_W), lambda i: (i, 0)),
        out_shape=jax.ShapeDtypeStruct((n, MIX_W), BF16),
        compiler_params=_cparams("parallel"),
        name="band_attention",
    )(h, h, h, h, h, h, h, bias_full)


def _band_bias(rel_bias):
    t = jnp.arange(BAND_TQ)[:, None] + BAND_PREV
    s = jnp.arange(BAND_WIN)[None, :]
    idx = jnp.clip(t - s, -(BAND_CHUNK - 1), BAND_REL_MAX) + (BAND_CHUNK - 1)
    tc, sc = t // BAND_CHUNK, s // BAND_CHUNK
    in_band = jnp.logical_and(sc <= tc, sc >= tc - BAND_LEFT)
    return jnp.where(in_band[None], rel_bias.astype(F32)[:, idx], NEG_BIG)


def _ssd_kernel(z_ref, xs_ref, bc_ref, dt_ref, cwx_ref, cbx_ref, cwb_ref, cbb_ref, dtb_ref,
                alog_ref, dsk_ref, ng_ref, o_ref, xpad_ref, bpad_ref, state_ref, *, q):
    c = pl.program_id(1)

    @pl.when(c == 0)
    def _():
        xpad_ref[0:8, :] = jnp.zeros((8, SSD_INNER), F32)
        bpad_ref[0:8, :] = jnp.zeros((8, SSD_BC), F32)
        state_ref[...] = jnp.zeros_like(state_ref)

    xpad_ref[8:8 + q, :] = xs_ref[...].astype(F32)
    bpad_ref[8:8 + q, :] = bc_ref[...].astype(F32)

    def conv_silu(pad_ref, w_ref, b_ref):
        acc = b_ref[...]
        for k in range(4):
            acc = acc + w_ref[k:k + 1, :] * pad_ref[5 + k:5 + k + q, :]
        return _silu(acc)

    xs = conv_silu(xpad_ref, cwx_ref, cbx_ref)
    bc = conv_silu(bpad_ref, cwb_ref, cbb_ref)
    xpad_ref[0:8, :] = xpad_ref[q:q + 8, :]
    bpad_ref[0:8, :] = bpad_ref[q:q + 8, :]
    bm = bc[:, 0:128].astype(BF16)
    cm = bc[:, 128:256].astype(BF16)

    r128 = lax.broadcasted_iota(jnp.int32, (128, SSD_INNER), 0)
    c512 = lax.broadcasted_iota(jnp.int32, (128, SSD_INNER), 1)
    expand = jnp.where(c512 // HEAD_DIM == r128, 1.0, 0.0).astype(BF16)
    pick = jnp.where(c512 == r128 * HEAD_DIM, 1.0, 0.0).astype(BF16)
    dt = _softplus(_dot(dt_ref[...], expand) + dtb_ref[...])
    da = dt * (-jnp.exp(alog_ref[...]))
    row = lax.broadcasted_iota(jnp.int32, (q, q), 0)
    col = lax.broadcasted_iota(jnp.int32, (q, q), 1)
    causal = col <= row
    tri = jnp.where(causal, 1.0, 0.0).astype(BF16)
    da_hi, da_lo = _split2(da)
    acs = _dot(tri, da_hi) + _dot(tri, da_lo)
    a_hi, a_mid, a_lo = _split3(acs)
    acs_t = _nt_dot(pick, a_hi) + _nt_dot(pick, a_mid) + _nt_dot(pick, a_lo)
    xdt = xs * dt

    eye = jnp.where(lax.broadcasted_iota(jnp.int32, (128, 128), 0)
                    == lax.broadcasted_iota(jnp.int32, (128, 128), 1), 1.0, 0.0).astype(BF16)
    bm_t = _nt_dot(eye, bm).astype(BF16)
    lane128 = lax.broadcasted_iota(jnp.int32, (1, 128), 1)
    lane256 = lax.broadcasted_iota(jnp.int32, (1, 256), 1)

    y_groups = []
    for g in range(2):
        gm = (lane128 // HEAD_DIM) == g
        cb = _nt_dot(jnp.where(gm, cm, jnp.zeros_like(cm)), bm)
        xg = xdt[:, g * 256:(g + 1) * 256].astype(BF16)
        ms, xstack = [], []
        for hh in range(4):
            head = g * 4 + hh
            seg = acs[:, head * HEAD_DIM:head * HEAD_DIM + 1] - acs_t[head:head + 1, :]
            decay = jnp.exp(jnp.where(causal, seg, NEG_BIG))
            ms.append((cb * decay).astype(BF16))
            xstack.append(jnp.where((lane256 // HEAD_DIM) == hh, xg, jnp.zeros_like(xg)))
        y_groups.append(_dot(jnp.concatenate(ms, axis=1), jnp.concatenate(xstack, axis=0)))
    y_diag = jnp.concatenate(y_groups, axis=1)

    state = state_ref[...]
    y_off = _dot(cm, state.astype(BF16)) * jnp.exp(acs)
    acs_last = acs[q - 1:q, :]
    xw = (xdt * jnp.exp(acs_last - acs)).astype(BF16)
    keep = (lax.broadcasted_iota(jnp.int32, (128, SSD_INNER), 0) // HEAD_DIM
            == lax.broadcasted_iota(jnp.int32, (128, SSD_INNER), 1) // 256)
    state_ref[...] = jnp.where(keep, state * jnp.exp(acs_last) + _dot(bm_t, xw), 0.0)

    y = y_diag + y_off + xs * dsk_ref[...]
    y = y * _silu(z_ref[...].astype(F32))
    outs = []
    for g in range(2):
        yg = y[:, g * 256:(g + 1) * 256]
        outs.append(yg * lax.rsqrt(jnp.mean(yg * yg, axis=-1, keepdims=True) + LN_EPS))
    o_ref[...] = (jnp.concatenate(outs, axis=1) * ng_ref[...]).astype(o_ref.dtype)


def _ssd(h, conv_wx, conv_bx, conv_wb, conv_bb, dt_bias_e, a_log_e, d_e, norm_g, bsz, seq, q):
    n = bsz * seq
    nc = seq // q

    def tok(width, colblk):
        return pl.BlockSpec((q, width), lambda b, c: (b * nc + c, colblk))

    def const(shape):
        return pl.BlockSpec(shape, lambda b, c: (0, 0))

    return pl.pallas_call(
        functools.partial(_ssd_kernel, q=q),
        grid=(bsz, nc),
        in_specs=[tok(SSD_INNER, COL_Z // SSD_INNER), tok(SSD_INNER, COL_XS // SSD_INNER),
                  tok(SSD_BC, COL_BC // SSD_BC), tok(128, COL_DT // 128),
                  const((4, SSD_INNER)), const((1, SSD_INNER)),
                  const((4, SSD_BC)), const((1, SSD_BC)),
                  const((1, SSD_INNER)), const((1, SSD_INNER)), const((1, SSD_INNER)),
                  const((1, SSD_INNER))],
        out_specs=pl.BlockSpec((q, SSD_INNER), lambda b, c: (b * nc + c, 0)),
        out_shape=jax.ShapeDtypeStruct((n, SSD_INNER), BF16),
        scratch_shapes=[pltpu.VMEM((q + 8, SSD_INNER), F32), pltpu.VMEM((q + 8, SSD_BC), F32),
                        pltpu.VMEM((128, SSD_INNER), F32)],
        compiler_params=_cparams("parallel", "arbitrary"),
        name="ssd",
    )(h, h, h, h, conv_wx, conv_bx, conv_wb, conv_bb, dt_bias_e, a_log_e, d_e, norm_g)


def _merge_kernel(x_ref, g_ref, ya_ref, yb_ref, yc_ref, yd_ref, wa_ref, wb_ref, wc_ref, wd_ref,
                  wo_ref, lg_ref, lb_ref, o_ref):
    merged = None
    for i, (y_ref, w_ref) in enumerate(((ya_ref, wa_ref), (yb_ref, wb_ref), (yc_ref, wc_ref),
                                        (yd_ref, wd_ref))):
        gate = jax.nn.sigmoid(g_ref[:, i * D_MODEL:(i + 1) * D_MODEL].astype(F32))
        term = gate * _dot(y_ref[...], w_ref[...])
        merged = term if merged is None else merged + term
    o = DEEPNORM_ALPHA * x_ref[...] + _dot(merged.astype(BF16), wo_ref[...])
    o_ref[...] = _layer_norm(o, lg_ref[...], lb_ref[...])


def _merge(x2, h, ya, yb, yc, yd, wa, wb, wc, wd, wo, lg, lb, tm):
    n = x2.shape[0]

    def tok(width):
        return pl.BlockSpec((tm, width), lambda i: (i, 0))

    def const(shape):
        return pl.BlockSpec(shape, lambda i: (0, 0))

    return pl.pallas_call(
        _merge_kernel,
        grid=(n // tm,),
        in_specs=[tok(D_MODEL), tok(4 * D_MODEL), tok(MIX_W), tok(MIX_W), tok(MIX_W),
                  tok(SSD_INNER), const((MIX_W, D_MODEL)), const((MIX_W, D_MODEL)),
                  const((MIX_W, D_MODEL)), const((SSD_INNER, D_MODEL)),
                  const((D_MODEL, D_MODEL)), const((1, D_MODEL)), const((1, D_MODEL))],
        out_specs=tok(D_MODEL),
        out_shape=jax.ShapeDtypeStruct((n, D_MODEL), F32),
        compiler_params=_cparams("parallel"),
        name="merge_ln1",
    )(x2, h, ya, yb, yc, yd, wa, wb, wc, wd, wo, lg, lb)


def _router_kernel(x_ref, w_ref, c_ref):
    logits = jnp.dot(x_ref[...], w_ref[...], preferred_element_type=F32,
                     precision=lax.Precision.HIGHEST)
    lane = lax.broadcasted_iota(jnp.int32, logits.shape, 1)
    lg = jnp.where(lane < N_EXPERTS, logits, -jnp.inf)
    m1 = jnp.max(lg, axis=-1, keepdims=True)
    i1 = jnp.min(jnp.where(lg == m1, lane, 128), axis=-1, keepdims=True)
    first = lane == i1
    lg2 = jnp.where(first, -jnp.inf, lg)
    m2 = jnp.max(lg2, axis=-1, keepdims=True)
    i2 = jnp.min(jnp.where(lg2 == m2, lane, 128), axis=-1, keepdims=True)
    second = lane == i2
    e2 = jnp.exp(m2 - m1)
    denom = 1.0 + e2
    c_ref[...] = jnp.where(first, 1.0 / denom, 0.0) + jnp.where(second, e2 / denom, 0.0)


def _router(x2, w_router_padded, tm):
    n = x2.shape[0]
    return pl.pallas_call(
        _router_kernel,
        grid=(n // tm,),
        in_specs=[pl.BlockSpec((tm, D_MODEL), lambda i: (i, 0)),
                  pl.BlockSpec((D_MODEL, 128), lambda i: (0, 0))],
        out_specs=pl.BlockSpec((tm, 128), lambda i: (i, 0)),
        out_shape=jax.ShapeDtypeStruct((n, 128), F32),
        compiler_params=_cparams("parallel"),
        name="router",
    )(x2, w_router_padded)


def _ffn_kernel(x_ref, c_ref, wg_ref, wu_ref, wd_ref, p_ref, pg_ref, pp_ref, lg_ref, lb_ref,
                o_ref, acc_ref, xb_ref, *, use_combine):
    e = pl.program_id(1)
    f = pl.program_id(2)
    first = jnp.logical_and(e == 0, f == 0)
    last = jnp.logical_and(e == pl.num_programs(1) - 1, f == pl.num_programs(2) - 1)

    @pl.when(first)
    def _():
        acc_ref[...] = jnp.zeros_like(acc_ref)
        xb_ref[...] = x_ref[...].astype(BF16)

    xb = xb_ref[...]
    hid = _silu(_dot(xb, wg_ref[...])) * _dot(xb, wu_ref[...])
    if use_combine:
        c = c_ref[...]
        lane = lax.broadcasted_iota(jnp.int32, c.shape, 1)
        hid = hid * jnp.sum(jnp.where(lane == e, c, 0.0), axis=-1, keepdims=True)
    acc_ref[...] += _dot(hid.astype(BF16), wd_ref[...])

    @pl.when(last)
    def _():
        ple = (jax.nn.sigmoid(_dot(xb, pg_ref[...]))
               * _dot(p_ref[...].astype(BF16), pp_ref[...]))
        o = DEEPNORM_ALPHA * x_ref[...] + acc_ref[...] + ple
        o_ref[...] = _layer_norm(o, lg_ref[...], lb_ref[...])


def _ffn(x2, combine, wg, wu, wd, p2, pg, pp, lg, lb, tm, tf, use_combine):
    n = x2.shape[0]
    n_exp, _, d_ff = wg.shape
    ple_dim = p2.shape[1]

    def tok(width):
        return pl.BlockSpec((tm, width), lambda i, e, f: (i, 0))

    def const(shape):
        return pl.BlockSpec(shape, lambda i, e, f: (0, 0))

    return pl.pallas_call(
        functools.partial(_ffn_kernel, use_combine=use_combine),
        grid=(n // tm, n_exp, d_ff // tf),
        in_specs=[tok(D_MODEL), tok(128),
                  pl.BlockSpec((None, D_MODEL, tf), lambda i, e, f: (e, 0, f)),
                  pl.BlockSpec((None, D_MODEL, tf), lambda i, e, f: (e, 0, f)),
                  pl.BlockSpec((None, tf, D_MODEL), lambda i, e, f: (e, f, 0)),
                  tok(ple_dim), const((D_MODEL, D_MODEL)), const((ple_dim, D_MODEL)),
                  const((1, D_MODEL)), const((1, D_MODEL))],
        out_specs=tok(D_MODEL),
        out_shape=jax.ShapeDtypeStruct((n, D_MODEL), F32),
        scratch_shapes=[pltpu.VMEM((tm, D_MODEL), F32), pltpu.VMEM((tm, D_MODEL), BF16)],
        compiler_params=_cparams("parallel", "arbitrary", "arbitrary"),
        name="ffn_ple_ln2",
    )(x2, combine, wg, wu, wd, p2, pg, pp, lg, lb)


def _prep_w_in(w):
    a, b, c, d, g = jnp.split(w, [768, 1280, 2048, 3336], axis=1)
    z, xs, bc, dt = jnp.split(d, [512, 1024, 1280], axis=1)

    def scale_q(t):
        return jnp.concatenate([t[:, :MIX_W] * (HEAD_DIM ** -0.5), t[:, MIX_W:]], axis=1)

    pad = jnp.zeros((w.shape[0], IN_COLS - (COL_DT + 8)), w.dtype)
    return jnp.concatenate([g, scale_q(a), b, scale_q(c), z, xs, bc, dt, pad], axis=1).astype(BF16)


def _row(v):
    return v.reshape(1, -1).astype(F32)


def _per_head(v):
    return _row(jnp.repeat(v, HEAD_DIM))


def kernel(x, p, w_in, w_br_a, w_br_b, w_br_c, w_br_d, w_out, sg_ln_g, sg_ln_b, sg_w, sg_b,
           ca_rel_bias, ssd_conv_w, ssd_conv_b, ssd_dt_bias, ssd_a_log, ssd_d, ssd_norm_g,
           ln1_g, ln1_b, ffn_w_gate, ffn_w_up, ffn_w_down, moe_router, moe_w_gate, moe_w_up,
           moe_w_down, ple_w_gate, ple_w_proj, ln2_g, ln2_b):
    bsz, seq, _ = x.shape
    n = bsz * seq
    x2 = x.reshape(n, D_MODEL)
    tm_proj = min(1024, n)
    tm = min(512, n)
    sb_blk = min(256, seq)
    ssd_q = min(256, seq)
    sg_tb = min(1024, seq)

    for i in range(DEPTH):
        h = _inproj(x2, _prep_w_in(w_in[i]), tm_proj, 1536)
        ya = _stick_breaking(h, bsz, seq, sb_blk)
        sg_bias = jnp.repeat(jnp.transpose(sg_b[i]), HEAD_DIM, axis=1).astype(F32)
        yb = _spatial_gating(h, _row(sg_ln_g[i]), _row(sg_ln_b[i]), sg_w[i], sg_bias, sg_tb)
        yc = _band_attention(h, _band_bias(ca_rel_bias[i]), seq)
        cw, cb = ssd_conv_w[i], ssd_conv_b[i]
        yd = _ssd(h, cw[:, :SSD_INNER], _row(cb[:SSD_INNER]), cw[:, SSD_INNER:],
                  _row(cb[SSD_INNER:]), _per_head(ssd_dt_bias[i]), _per_head(ssd_a_log[i]),
                  _per_head(ssd_d[i]), _row(ssd_norm_g[i]), bsz, seq, ssd_q)
        x2 = _merge(x2, h, ya, yb, yc, yd, w_br_a[i].astype(BF16), w_br_b[i].astype(BF16),
                    w_br_c[i].astype(BF16), w_br_d[i].astype(BF16), w_out[i].astype(BF16),
                    _row(ln1_g[i]), _row(ln1_b[i]), tm)
        p2 = p[i].reshape(n, -1)
        pg, pp = ple_w_gate[i].astype(BF16), ple_w_proj[i].astype(BF16)
        if i % 2 == 0:
            j = i // 2
            ones = jnp.ones((n, 128), F32)
            x2 = _ffn(x2, ones, ffn_w_gate[j:j + 1].astype(BF16), ffn_w_up[j:j + 1].astype(BF16),
                      ffn_w_down[j:j + 1].astype(BF16), p2, pg, pp, _row(ln2_g[i]),
                      _row(ln2_b[i]), tm, 1408, False)
        else:
            j = i // 2
            wr = jnp.pad(moe_router[j], ((0, 0), (0, 128 - N_EXPERTS)))
            combine = _router(x2, wr, tm)
            x2 = _ffn(x2, combine, moe_w_gate[j].astype(BF16), moe_w_up[j].astype(BF16),
                      moe_w_down[j].astype(BF16), p2, pg, pp, _row(ln2_g[i]), _row(ln2_b[i]),
                      tm, 896, True)
    return x2.reshape(bsz, seq, D_MODEL)
```

```python
import functools

import jax
import jax.numpy as jnp
from jax import lax
from jax.experimental import pallas as pl
from jax.experimental.pallas import tpu as pltpu

F32 = jnp.float32
BF16 = jnp.bfloat16

D_MODEL = 1024
DEPTH = 2
LN_EPS = 1e-5
DEEPNORM_ALPHA = (2 * DEPTH) ** 0.25
HEAD_DIM = 64
MIX_W = 256
SG_CHUNK = 128
BAND_CHUNK = 64
BAND_LEFT = 8
BAND_REL_MAX = 256
SSD_INNER = 512
SSD_HEADS = 8
SSD_BC = 256
N_EXPERTS = 8
NEG_BIG = -1e30
EXP_UNDERFLOW = -110.0

COL_GATE = 0
COL_A = 4096
COL_B = 4864
COL_C = 5376
COL_Z = 6144
COL_XS = 6656
COL_BC = 7168
COL_DT = 7424
IN_COLS = 7680

VMEM_LIMIT = 56 * 1024 * 1024


def _cparams(*sem):
    return pltpu.CompilerParams(dimension_semantics=sem, vmem_limit_bytes=VMEM_LIMIT)


def _nt_dot(a, b):
    return lax.dot_general(a, b, (((1,), (1,)), ((), ())), preferred_element_type=F32)


def _dot(a, b):
    return jnp.dot(a, b, preferred_element_type=F32)


def _softplus(x):
    return jnp.maximum(x, 0.0) + jnp.log(1.0 + jnp.exp(-jnp.abs(x)))


def _silu(x):
    return x * jax.nn.sigmoid(x)


def _split2(x):
    hi = x.astype(BF16)
    lo = (x - hi.astype(F32)).astype(BF16)
    return hi, lo


def _split3(x):
    hi = x.astype(BF16)
    r = x - hi.astype(F32)
    mid = r.astype(BF16)
    lo = (r - mid.astype(F32)).astype(BF16)
    return hi, mid, lo


def _layer_norm(x, g, b):
    mu = jnp.mean(x, axis=-1, keepdims=True)
    xc = x - mu
    var = jnp.mean(xc * xc, axis=-1, keepdims=True)
    return xc * lax.rsqrt(var + LN_EPS) * g + b


def _inproj_kernel(x_ref, w_ref, o_ref):
    o_ref[...] = _dot(x_ref[...].astype(BF16), w_ref[...]).astype(o_ref.dtype)


def _inproj(x2, w, tm, tn):
    n, d = x2.shape
    nc = w.shape[1]
    return pl.pallas_call(
        _inproj_kernel,
        grid=(n // tm, nc // tn),
        in_specs=[pl.BlockSpec((tm, d), lambda i, j: (i, 0)),
                  pl.BlockSpec((d, tn), lambda i, j: (0, j))],
        out_specs=pl.BlockSpec((tm, tn), lambda i, j: (i, j)),
        out_shape=jax.ShapeDtypeStruct((n, nc), BF16),
        compiler_params=_cparams("parallel", "arbitrary"),
        name="inproj",
    )(x2, w)


def _sb_kernel(q_ref, k_ref, v_ref, o_ref, acc_ref, run_ref, kmax_ref, *, blk, seq):
    qi = pl.program_id(1)
    lane = lax.broadcasted_iota(jnp.int32, (1, MIX_W), 1)
    head_mask = [(lane // HEAD_DIM) == h for h in range(4)]

    def head_sq_norms(t):
        tf = t.astype(F32)
        sq = tf * tf
        return [jnp.sum(jnp.where(head_mask[h], sq, 0.0), axis=-1, keepdims=True)
                for h in range(4)]

    @pl.when(qi == 0)
    def _():
        kmax_ref[...] = jnp.zeros_like(kmax_ref)

        def scan(c, carry):
            norms = head_sq_norms(k_ref[pl.ds(pl.multiple_of(c * blk, blk), blk), :])
            for h in range(4):
                kmax_ref[h] = jnp.maximum(kmax_ref[h], jnp.max(norms[h], axis=0, keepdims=True))
            return carry

        lax.fori_loop(0, seq // blk, scan, 0)

    q = q_ref[...]
    q_heads = [jnp.where(head_mask[h], q, jnp.zeros_like(q)) for h in range(4)]
    q_norms = head_sq_norms(q)
    z_bound = [jnp.sqrt(q_norms[h] * kmax_ref[h]) * 1.001 + 1e-3 for h in range(4)]
    row = lax.broadcasted_iota(jnp.int32, (blk, blk), 0)
    col = lax.broadcasted_iota(jnp.int32, (blk, blk), 1)
    below_diag = col < row
    suffix = jnp.where(row >= col, 1.0, 0.0).astype(BF16)

    acc_ref[...] = jnp.zeros_like(acc_ref)
    run_ref[...] = jnp.zeros_like(run_ref)

    def cond(carry):
        j, live = carry
        return jnp.logical_and(j <= qi, live)

    def body(carry):
        j, _ = carry
        kb = qi - j
        start = pl.multiple_of(kb * blk, blk)
        k_blk = k_ref[pl.ds(start, blk), :]
        v_blk = v_ref[pl.ds(start, blk), :]
        valid = jnp.logical_or(below_diag, j > 0)
        weights = []
        slack = None
        for h in range(4):
            z = _nt_dot(q_heads[h], k_blk)
            sp = jnp.where(valid, _softplus(z), 0.0)
            hi, lo = _split2(sp)
            cs = _dot(hi, suffix) + _dot(lo, suffix)
            run = run_ref[h]
            w = jnp.exp(jnp.where(valid, z - cs - run, NEG_BIG))
            run = run + cs[:, 0:1]
            run_ref[h] = run
            weights.append(w.astype(BF16))
            head_slack = jnp.max(z_bound[h] - run)
            slack = head_slack if slack is None else jnp.maximum(slack, head_slack)
        wcat = jnp.concatenate(weights, axis=1)
        vcat = jnp.concatenate(
            [jnp.where(head_mask[h], v_blk, jnp.zeros_like(v_blk)) for h in range(4)], axis=0)
        acc_ref[...] += _dot(wcat, vcat)
        return j + 1, slack > EXP_UNDERFLOW

    lax.while_loop(cond, body, (jnp.int32(0), jnp.bool_(True)))
    o_ref[...] = acc_ref[...].astype(o_ref.dtype)


def _stick_breaking(h, bsz, seq, blk):
    n = bsz * seq
    nq = seq // blk
    cq, ck, cv = (COL_A // MIX_W + i for i in range(3))
    return pl.pallas_call(
        functools.partial(_sb_kernel, blk=blk, seq=seq),
        grid=(bsz, nq),
        in_specs=[pl.BlockSpec((blk, MIX_W), lambda b, i: (b * nq + i, cq)),
                  pl.BlockSpec((seq, MIX_W), lambda b, i: (b, ck)),
                  pl.BlockSpec((seq, MIX_W), lambda b, i: (b, cv))],
        out_specs=pl.BlockSpec((blk, MIX_W), lambda b, i: (b * nq + i, 0)),
        out_shape=jax.ShapeDtypeStruct((n, MIX_W), BF16),
        scratch_shapes=[pltpu.VMEM((blk, MIX_W), F32), pltpu.VMEM((4, blk, 1), F32),
                        pltpu.VMEM((4, 1, 1), F32)],
        compiler_params=_cparams("parallel", "arbitrary"),
        name="stick_breaking",
    )(h, h, h)


def _gelu_tanh(x):
    return 0.5 * x * (1.0 + jnp.tanh(0.7978845608028654 * (x + 0.044715 * (x * x * x))))


def _sg_kernel(u_ref, v_ref, g_ref, b_ref, w_ref, bias_ref, o_ref, *, tb):
    u = _gelu_tanh(u_ref[...].astype(F32))
    v = _gelu_tanh(v_ref[...].astype(F32))
    vn = _layer_norm(v, g_ref[...], b_ref[...]).astype(BF16)
    lane = lax.broadcasted_iota(jnp.int32, (1, MIX_W), 1)
    group_mask = [(lane // HEAD_DIM) == g for g in range(4)]
    row = lax.broadcasted_iota(jnp.int32, (SG_CHUNK, SG_CHUNK), 0)
    col = lax.broadcasted_iota(jnp.int32, (SG_CHUNK, SG_CHUNK), 1)
    causal = col <= row
    wcat = jnp.concatenate(
        [jnp.where(causal, w_ref[g], 0.0).astype(BF16) for g in range(4)], axis=1)
    bias = bias_ref[...]
    for c in range(tb // SG_CHUNK):
        sl = slice(c * SG_CHUNK, (c + 1) * SG_CHUNK)
        vc = vn[sl, :]
        vstack = jnp.concatenate(
            [jnp.where(group_mask[g], vc, jnp.zeros_like(vc)) for g in range(4)], axis=0)
        mixed = _dot(wcat, vstack) + bias
        o_ref[sl, :] = (u[sl, :] * mixed).astype(o_ref.dtype)


def _spatial_gating(h, ln_g, ln_b, w_s, bias_full, tb):
    n = h.shape[0]
    cu, cv = COL_B // MIX_W, COL_B // MIX_W + 1
    return pl.pallas_call(
        functools.partial(_sg_kernel, tb=tb),
        grid=(n // tb,),
        in_specs=[pl.BlockSpec((tb, MIX_W), lambda i: (i, cu)),
                  pl.BlockSpec((tb, MIX_W), lambda i: (i, cv)),
                  pl.BlockSpec((1, MIX_W), lambda i: (0, 0)),
                  pl.BlockSpec((1, MIX_W), lambda i: (0, 0)),
                  pl.BlockSpec((4, SG_CHUNK, SG_CHUNK), lambda i: (0, 0, 0)),
                  pl.BlockSpec((SG_CHUNK, MIX_W), lambda i: (0, 0))],
        out_specs=pl.BlockSpec((tb, MIX_W), lambda i: (i, 0)),
        out_shape=jax.ShapeDtypeStruct((n, MIX_W), BF16),
        compiler_params=_cparams("parallel"),
        name="spatial_gating",
    )(h, h, ln_g, ln_b, w_s, bias_full)


BAND_TQ = 256
BAND_PREV = BAND_LEFT * BAND_CHUNK
BAND_WIN = BAND_PREV + BAND_TQ


def _band_kernel(q_ref, k2_ref, k1_ref, k0_ref, v2_ref, v1_ref, v0_ref, bias_ref, o_ref, *, nblk):
    bi = pl.program_id(0) % nblk
    lane = lax.broadcasted_iota(jnp.int32, (1, MIX_W), 1)
    q = q_ref[...]
    kcat = jnp.concatenate([k2_ref[...], k1_ref[...], k0_ref[...]], axis=0)
    vcat = jnp.concatenate([v2_ref[...], v1_ref[...], v0_ref[...]], axis=0)
    col = lax.broadcasted_iota(jnp.int32, (1, BAND_WIN), 1)
    in_seq = col >= (2 - jnp.minimum(bi, 2)) * BAND_TQ
    out = jnp.zeros((BAND_TQ, MIX_W), F32)
    for h in range(4):
        hm = (lane // HEAD_DIM) == h
        s = _nt_dot(jnp.where(hm, q, jnp.zeros_like(q)), kcat) + bias_ref[h]
        s = jnp.where(in_seq, s, NEG_BIG)
        m = jnp.max(s, axis=-1, keepdims=True)
        p = jnp.exp(s - m)
        l = jnp.sum(p, axis=-1, keepdims=True)
        o = _dot(p.astype(BF16), vcat) / l
        out = jnp.where(hm, o, out)
    o_ref[...] = out.astype(o_ref.dtype)


def _band_attention(h, bias_full, seq):
    n = h.shape[0]
    nblk = seq // BAND_TQ
    cq, ck, cv = (COL_C // MIX_W + i for i in range(3))

    def prev(i, d):
        return i - jnp.minimum(i % nblk, d)

    def spec(c, d):
        return pl.BlockSpec((BAND_TQ, MIX_W), lambda i: (prev(i, d), c))

    return pl.pallas_call(
        functools.partial(_band_kernel, nblk=nblk),
        grid=(n // BAND_TQ,),
        in_specs=[spec(cq, 0), spec(ck, 2), spec(ck, 1), spec(ck, 0),
                  spec(cv, 2), spec(cv, 1), spec(cv, 0),
                  pl.BlockSpec((4, BAND_TQ, BAND_WIN), lambda i: (0, 0, 0))],
        out_specs=pl.BlockSpec((BAND_TQ, MIX_W), lambda i: (i, 0)),
        out_shape=jax.ShapeDtypeStruct((n, MIX_W), BF16),
        compiler_params=_cparams("parallel"),
        name="band_attention",
    )(h, h, h, h, h, h, h, bias_full)


def _band_bias(rel_bias):
    period = 1024
    u = jnp.arange(period)
    rel = jnp.where(u <= BAND_WIN, BAND_PREV - u, BAND_PREV + period - u)
    row0 = rel_bias.astype(F32)[:, jnp.clip(rel, -(BAND_CHUNK - 1), BAND_REL_MAX) + (BAND_CHUNK - 1)]
    skew = jnp.tile(row0, (1, BAND_TQ))[:, :BAND_TQ * (period - 1)]
    bias = skew.reshape(4, BAND_TQ, period - 1)[:, :, :BAND_WIN]
    tc = (jnp.arange(BAND_TQ)[:, None] + BAND_PREV) // BAND_CHUNK
    sc = jnp.arange(BAND_WIN)[None, :] // BAND_CHUNK
    in_band = jnp.logical_and(sc <= tc, sc >= tc - BAND_LEFT)
    return jnp.where(in_band[None], bias, NEG_BIG)


def _ssd_kernel(z_ref, xs_ref, bc_ref, dt_ref, cwx_ref, cbx_ref, cwb_ref, cbb_ref, dtb_ref,
                alog_ref, dsk_ref, ng_ref, o_ref, xpad_ref, bpad_ref, state_ref, *, q):
    c = pl.program_id(1)

    @pl.when(c == 0)
    def _():
        xpad_ref[0:8, :] = jnp.zeros((8, SSD_INNER), F32)
        bpad_ref[0:8, :] = jnp.zeros((8, SSD_BC), F32)
        state_ref[...] = jnp.zeros_like(state_ref)

    xpad_ref[8:8 + q, :] = xs_ref[...].astype(F32)
    bpad_ref[8:8 + q, :] = bc_ref[...].astype(F32)

    def conv_silu(pad_ref, w_ref, b_ref):
        acc = b_ref[...]
        for k in range(4):
            acc = acc + w_ref[k:k + 1, :] * pad_ref[5 + k:5 + k + q, :]
        return _silu(acc)

    xs = conv_silu(xpad_ref, cwx_ref, cbx_ref)
    bc = conv_silu(bpad_ref, cwb_ref, cbb_ref)
    xpad_ref[0:8, :] = xpad_ref[q:q + 8, :]
    bpad_ref[0:8, :] = bpad_ref[q:q + 8, :]
    bm = bc[:, 0:128].astype(BF16)
    cm = bc[:, 128:256].astype(BF16)

    r128 = lax.broadcasted_iota(jnp.int32, (128, SSD_INNER), 0)
    c512 = lax.broadcasted_iota(jnp.int32, (128, SSD_INNER), 1)
    expand = jnp.where(c512 // HEAD_DIM == r128, 1.0, 0.0).astype(BF16)
    pick = jnp.where(c512 == r128 * HEAD_DIM, 1.0, 0.0).astype(BF16)
    dt = _softplus(_dot(dt_ref[...], expand) + dtb_ref[...])
    da = dt * (-jnp.exp(alog_ref[...]))
    row = lax.broadcasted_iota(jnp.int32, (q, q), 0)
    col = lax.broadcasted_iota(jnp.int32, (q, q), 1)
    causal = col <= row
    tri = jnp.where(causal, 1.0, 0.0).astype(BF16)
    da_hi, da_lo = _split2(da)
    acs = _dot(tri, da_hi) + _dot(tri, da_lo)
    a_hi, a_mid, a_lo = _split3(acs)
    acs_t = _nt_dot(pick, a_hi) + _nt_dot(pick, a_mid) + _nt_dot(pick, a_lo)
    xdt = xs * dt

    eye = jnp.where(lax.broadcasted_iota(jnp.int32, (128, 128), 0)
                    == lax.broadcasted_iota(jnp.int32, (128, 128), 1), 1.0, 0.0).astype(BF16)
    bm_t = _nt_dot(eye, bm).astype(BF16)
    lane128 = lax.broadcasted_iota(jnp.int32, (1, 128), 1)
    lane256 = lax.broadcasted_iota(jnp.int32, (1, 256), 1)

    y_groups = []
    for g in range(2):
        gm = (lane128 // HEAD_DIM) == g
        cb = _nt_dot(jnp.where(gm, cm, jnp.zeros_like(cm)), bm)
        xg = xdt[:, g * 256:(g + 1) * 256].astype(BF16)
        ms, xstack = [], []
        for hh in range(4):
            head = g * 4 + hh
            seg = acs[:, head * HEAD_DIM:head * HEAD_DIM + 1] - acs_t[head:head + 1, :]
            decay = jnp.exp(jnp.where(causal, seg, NEG_BIG))
            ms.append((cb * decay).astype(BF16))
            xstack.append(jnp.where((lane256 // HEAD_DIM) == hh, xg, jnp.zeros_like(xg)))
        y_groups.append(_dot(jnp.concatenate(ms, axis=1), jnp.concatenate(xstack, axis=0)))
    y_diag = jnp.concatenate(y_groups, axis=1)

    state = state_ref[...]
    y_off = _dot(cm, state.astype(BF16)) * jnp.exp(acs)
    acs_last = acs[q - 1:q, :]
    xw = (xdt * jnp.exp(acs_last - acs)).astype(BF16)
    keep = (lax.broadcasted_iota(jnp.int32, (128, SSD_INNER), 0) // HEAD_DIM
            == lax.broadcasted_iota(jnp.int32, (128, SSD_INNER), 1) // 256)
    state_ref[...] = jnp.where(keep, state * jnp.exp(acs_last) + _dot(bm_t, xw), 0.0)

    y = y_diag + y_off + xs * dsk_ref[...]
    y = y * _silu(z_ref[...].astype(F32))
    outs = []
    for g in range(2):
        yg = y[:, g * 256:(g + 1) * 256]
        outs.append(yg * lax.rsqrt(jnp.mean(yg * yg, axis=-1, keepdims=True) + LN_EPS))
    o_ref[...] = (jnp.concatenate(outs, axis=1) * ng_ref[...]).astype(o_ref.dtype)


def _ssd(h, conv_wx, conv_bx, conv_wb, conv_bb, dt_bias_e, a_log_e, d_e, norm_g, bsz, seq, q):
    n = bsz * seq
    nc = seq // q

    def tok(width, colblk):
        return pl.BlockSpec((q, width), lambda b, c: (b * nc + c, colblk))

    def const(shape):
        return pl.BlockSpec(shape, lambda b, c: (0, 0))

    return pl.pallas_call(
        functools.partial(_ssd_kernel, q=q),
        grid=(bsz, nc),
        in_specs=[tok(SSD_INNER, COL_Z // SSD_INNER), tok(SSD_INNER, COL_XS // SSD_INNER),
                  tok(SSD_BC, COL_BC // SSD_BC), tok(128, COL_DT // 128),
                  const((4, SSD_INNER)), const((1, SSD_INNER)),
                  const((4, SSD_BC)), const((1, SSD_BC)),
                  const((1, SSD_INNER)), const((1, SSD_INNER)), const((1, SSD_INNER)),
                  const((1, SSD_INNER))],
        out_specs=pl.BlockSpec((q, SSD_INNER), lambda b, c: (b * nc + c, 0)),
        out_shape=jax.ShapeDtypeStruct((n, SSD_INNER), BF16),
        scratch_shapes=[pltpu.VMEM((q + 8, SSD_INNER), F32), pltpu.VMEM((q + 8, SSD_BC), F32),
                        pltpu.VMEM((128, SSD_INNER), F32)],
        compiler_params=_cparams("parallel", "arbitrary"),
        name="ssd",
    )(h, h, h, h, conv_wx, conv_bx, conv_wb, conv_bb, dt_bias_e, a_log_e, d_e, norm_g)


def _merge_kernel(x_ref, g_ref, ya_ref, yb_ref, yc_ref, yd_ref, wa_ref, wb_ref, wc_ref, wd_ref,
                  wo_ref, lg_ref, lb_ref, o_ref):
    merged = None
    for i, (y_ref, w_ref) in enumerate(((ya_ref, wa_ref), (yb_ref, wb_ref), (yc_ref, wc_ref),
                                        (yd_ref, wd_ref))):
        gate = jax.nn.sigmoid(g_ref[:, i * D_MODEL:(i + 1) * D_MODEL].astype(F32))
        term = gate * _dot(y_ref[...], w_ref[...])
        merged = term if merged is None else merged + term
    o = DEEPNORM_ALPHA * x_ref[...] + _dot(merged.astype(BF16), wo_ref[...])
    o_ref[...] = _layer_norm(o, lg_ref[...], lb_ref[...])


def _merge(x2, h, ya, yb, yc, yd, wa, wb, wc, wd, wo, lg, lb, tm):
    n = x2.shape[0]

    def tok(width):
        return pl.BlockSpec((tm, width), lambda i: (i, 0))

    def const(shape):
        return pl.BlockSpec(shape, lambda i: (0, 0))

    return pl.pallas_call(
        _merge_kernel,
        grid=(n // tm,),
        in_specs=[tok(D_MODEL), tok(4 * D_MODEL), tok(MIX_W), tok(MIX_W), tok(MIX_W),
                  tok(SSD_INNER), const((MIX_W, D_MODEL)), const((MIX_W, D_MODEL)),
                  const((MIX_W, D_MODEL)), const((SSD_INNER, D_MODEL)),
                  const((D_MODEL, D_MODEL)), const((1, D_MODEL)), const((1, D_MODEL))],
        out_specs=tok(D_MODEL),
        out_shape=jax.ShapeDtypeStruct((n, D_MODEL), F32),
        compiler_params=_cparams("parallel"),
        name="merge_ln1",
    )(x2, h, ya, yb, yc, yd, wa, wb, wc, wd, wo, lg, lb)


def _router_kernel(x_ref, w_ref, c_ref):
    logits = jnp.dot(x_ref[...], w_ref[...], preferred_element_type=F32,
                     precision=lax.Precision.HIGHEST)
    lane = lax.broadcasted_iota(jnp.int32, logits.shape, 1)
    lg = jnp.where(lane < N_EXPERTS, logits, -jnp.inf)
    m1 = jnp.max(lg, axis=-1, keepdims=True)
    i1 = jnp.min(jnp.where(lg == m1, lane, 128), axis=-1, keepdims=True)
    first = lane == i1
    lg2 = jnp.where(first, -jnp.inf, lg)
    m2 = jnp.max(lg2, axis=-1, keepdims=True)
    i2 = jnp.min(jnp.where(lg2 == m2, lane, 128), axis=-1, keepdims=True)
    second = lane == i2
    e2 = jnp.exp(m2 - m1)
    denom = 1.0 + e2
    c_ref[...] = jnp.where(first, 1.0 / denom, 0.0) + jnp.where(second, e2 / denom, 0.0)


def _router(x2, w_router_padded, tm):
    n = x2.shape[0]
    return pl.pallas_call(
        _router_kernel,
        grid=(n // tm,),
        in_specs=[pl.BlockSpec((tm, D_MODEL), lambda i: (i, 0)),
                  pl.BlockSpec((D_MODEL, 128), lambda i: (0, 0))],
        out_specs=pl.BlockSpec((tm, 128), lambda i: (i, 0)),
        out_shape=jax.ShapeDtypeStruct((n, 128), F32),
        compiler_params=_cparams("parallel"),
        name="router",
    )(x2, w_router_padded)


def _ffn_kernel(x_ref, c_ref, wg_ref, wu_ref, wd_ref, p_ref, pg_ref, pp_ref, lg_ref, lb_ref,
                o_ref, acc_ref, xb_ref, *, use_combine):
    e = pl.program_id(1)
    f = pl.program_id(2)
    first = jnp.logical_and(e == 0, f == 0)
    last = jnp.logical_and(e == pl.num_programs(1) - 1, f == pl.num_programs(2) - 1)

    @pl.when(first)
    def _():
        acc_ref[...] = jnp.zeros_like(acc_ref)
        xb_ref[...] = x_ref[...].astype(BF16)

    xb = xb_ref[...]
    hid = _silu(_dot(xb, wg_ref[...])) * _dot(xb, wu_ref[...])
    if use_combine:
        c = c_ref[...]
        lane = lax.broadcasted_iota(jnp.int32, c.shape, 1)
        hid = hid * jnp.sum(jnp.where(lane == e, c, 0.0), axis=-1, keepdims=True)
    acc_ref[...] += _dot(hid.astype(BF16), wd_ref[...])

    @pl.when(last)
    def _():
        ple = (jax.nn.sigmoid(_dot(xb, pg_ref[...]))
               * _dot(p_ref[...].astype(BF16), pp_ref[...]))
        o = DEEPNORM_ALPHA * x_ref[...] + acc_ref[...] + ple
        o_ref[...] = _layer_norm(o, lg_ref[...], lb_ref[...])


def _ffn(x2, combine, wg, wu, wd, p2, pg, pp, lg, lb, tm, tf, use_combine):
    n = x2.shape[0]
    n_exp, _, d_ff = wg.shape
    ple_dim = p2.shape[1]

    def tok(width):
        return pl.BlockSpec((tm, width), lambda i, e, f: (i, 0))

    def const(shape):
        return pl.BlockSpec(shape, lambda i, e, f: (0, 0))

    return pl.pallas_call(
        functools.partial(_ffn_kernel, use_combine=use_combine),
        grid=(n // tm, n_exp, d_ff // tf),
        in_specs=[tok(D_MODEL), tok(128),
                  pl.BlockSpec((None, D_MODEL, tf), lambda i, e, f: (e, 0, f)),
                  pl.BlockSpec((None, D_MODEL, tf), lambda i, e, f: (e, 0, f)),
                  pl.BlockSpec((None, tf, D_MODEL), lambda i, e, f: (e, f, 0)),
                  tok(ple_dim), const((D_MODEL, D_MODEL)), const((ple_dim, D_MODEL)),
                  const((1, D_MODEL)), const((1, D_MODEL))],
        out_specs=tok(D_MODEL),
        out_shape=jax.ShapeDtypeStruct((n, D_MODEL), F32),
        scratch_shapes=[pltpu.VMEM((tm, D_MODEL), F32), pltpu.VMEM((tm, D_MODEL), BF16)],
        compiler_params=_cparams("parallel", "arbitrary", "arbitrary"),
        name="ffn_ple_ln2",
    )(x2, combine, wg, wu, wd, p2, pg, pp, lg, lb)


def _prep_w_in(w):
    a, b, c, d, g = jnp.split(w, [768, 1280, 2048, 3336], axis=1)
    z, xs, bc, dt = jnp.split(d, [512, 1024, 1280], axis=1)

    def scale_q(t):
        return jnp.concatenate([t[:, :MIX_W] * (HEAD_DIM ** -0.5), t[:, MIX_W:]], axis=1)

    pad = jnp.zeros((w.shape[0], IN_COLS - (COL_DT + 8)), w.dtype)
    return jnp.concatenate([g, scale_q(a), b, scale_q(c), z, xs, bc, dt, pad], axis=1).astype(BF16)


def _row(v):
    return v.reshape(1, -1).astype(F32)


def _per_head(v):
    return _row(jnp.repeat(v, HEAD_DIM))


def kernel(x, p, w_in, w_br_a, w_br_b, w_br_c, w_br_d, w_out, sg_ln_g, sg_ln_b, sg_w, sg_b,
           ca_rel_bias, ssd_conv_w, ssd_conv_b, ssd_dt_bias, ssd_a_log, ssd_d, ssd_norm_g,
           ln1_g, ln1_b, ffn_w_gate, ffn_w_up, ffn_w_down, moe_router, moe_w_gate, moe_w_up,
           moe_w_down, ple_w_gate, ple_w_proj, ln2_g, ln2_b):
    bsz, seq, _ = x.shape
    n = bsz * seq
    x2 = x.reshape(n, D_MODEL)
    tm_proj = min(1024, n)
    tm = min(512, n)
    sb_blk = min(256, seq)
    ssd_q = min(256, seq)
    sg_tb = min(1024, seq)

    for i in range(DEPTH):
        h = _inproj(x2, _prep_w_in(w_in[i]), tm_proj, 1536)
        ya = _stick_breaking(h, bsz, seq, sb_blk)
        sg_bias = jnp.repeat(jnp.transpose(sg_b[i]), HEAD_DIM, axis=1).astype(F32)
        yb = _spatial_gating(h, _row(sg_ln_g[i]), _row(sg_ln_b[i]), sg_w[i], sg_bias, sg_tb)
        yc = _band_attention(h, _band_bias(ca_rel_bias[i]), seq)
        cw, cb = ssd_conv_w[i], ssd_conv_b[i]
        yd = _ssd(h, cw[:, :SSD_INNER], _row(cb[:SSD_INNER]), cw[:, SSD_INNER:],
                  _row(cb[SSD_INNER:]), _per_head(ssd_dt_bias[i]), _per_head(ssd_a_log[i]),
                  _per_head(ssd_d[i]), _row(ssd_norm_g[i]), bsz, seq, ssd_q)
        x2 = _merge(x2, h, ya, yb, yc, yd, w_br_a[i].astype(BF16), w_br_b[i].astype(BF16),
                    w_br_c[i].astype(BF16), w_br_d[i].astype(BF16), w_out[i].astype(BF16),
                    _row(ln1_g[i]), _row(ln1_b[i]), tm)
        p2 = p[i].reshape(n, -1)
        pg, pp = ple_w_gate[i].astype(BF16), ple_w_proj[i].astype(BF16)
        if i % 2 == 0:
            j = i // 2
            ones = jnp.ones((n, 128), F32)
            x2 = _ffn(x2, ones, ffn_w_gate[j:j + 1].astype(BF16), ffn_w_up[j:j + 1].astype(BF16),
                      ffn_w_down[j:j + 1].astype(BF16), p2, pg, pp, _row(ln2_g[i]),
                      _row(ln2_b[i]), tm, 1408, False)
        else:
            j = i // 2
            wr = jnp.pad(moe_router[j], ((0, 0), (0, 128 - N_EXPERTS)))
            combine = _router(x2, wr, tm)
            x2 = _ffn(x2, combine, moe_w_gate[j].astype(BF16), moe_w_up[j].astype(BF16),
                      moe_w_down[j].astype(BF16), p2, pg, pp, _row(ln2_g[i]), _row(ln2_b[i]),
                      tm, 896, True)
    return x2.reshape(bsz, seq, D_MODEL)
```

```python
import functools

import jax
import jax.numpy as jnp
from jax import lax
from jax.experimental import pallas as pl
from jax.experimental.pallas import tpu as pltpu

F32 = jnp.float32
BF16 = jnp.bfloat16

D_MODEL = 1024
DEPTH = 2
LN_EPS = 1e-5
DEEPNORM_ALPHA = (2 * DEPTH) ** 0.25
HEAD_DIM = 64
MIX_W = 256
SG_CHUNK = 128
BAND_CHUNK = 64
BAND_LEFT = 8
BAND_REL_MAX = 256
SSD_INNER = 512
SSD_HEADS = 8
SSD_BC = 256
N_EXPERTS = 8
NEG_BIG = -1e30
EXP_UNDERFLOW = -110.0

COL_GATE = 0
COL_A = 4096
COL_B = 4864
COL_C = 5376
COL_Z = 6144
COL_XS = 6656
COL_BC = 7168
COL_DT = 7424
IN_COLS = 7680

VMEM_LIMIT = 56 * 1024 * 1024


def _cparams(*sem):
    return pltpu.CompilerParams(dimension_semantics=sem, vmem_limit_bytes=VMEM_LIMIT)


def _nt_dot(a, b):
    return lax.dot_general(a, b, (((1,), (1,)), ((), ())), preferred_element_type=F32)


def _dot(a, b):
    return jnp.dot(a, b, preferred_element_type=F32)


def _softplus(x):
    return jnp.maximum(x, 0.0) + jnp.log(1.0 + jnp.exp(-jnp.abs(x)))


def _silu(x):
    return x * jax.nn.sigmoid(x)


def _split2(x):
    hi = x.astype(BF16)
    lo = (x - hi.astype(F32)).astype(BF16)
    return hi, lo


def _split3(x):
    hi = x.astype(BF16)
    r = x - hi.astype(F32)
    mid = r.astype(BF16)
    lo = (r - mid.astype(F32)).astype(BF16)
    return hi, mid, lo


def _layer_norm(x, g, b):
    mu = jnp.mean(x, axis=-1, keepdims=True)
    xc = x - mu
    var = jnp.mean(xc * xc, axis=-1, keepdims=True)
    return xc * lax.rsqrt(var + LN_EPS) * g + b


def _inproj_kernel(x_ref, w_ref, o_ref):
    o_ref[...] = _dot(x_ref[...].astype(BF16), w_ref[...]).astype(o_ref.dtype)


def _inproj(x2, w, tm, tn):
    n, d = x2.shape
    nc = w.shape[1]
    return pl.pallas_call(
        _inproj_kernel,
        grid=(n // tm, nc // tn),
        in_specs=[pl.BlockSpec((tm, d), lambda i, j: (i, 0)),
                  pl.BlockSpec((d, tn), lambda i, j: (0, j))],
        out_specs=pl.BlockSpec((tm, tn), lambda i, j: (i, j)),
        out_shape=jax.ShapeDtypeStruct((n, nc), BF16),
        compiler_params=_cparams("parallel", "arbitrary"),
        name="inproj",
    )(x2, w)


def _sb_kernel(q_ref, k_ref, v_ref, o_ref, acc_ref, run_ref, kmax_ref, *, blk, seq):
    qi = pl.program_id(1)
    lane = lax.broadcasted_iota(jnp.int32, (1, MIX_W), 1)
    head_mask = [(lane // HEAD_DIM) == h for h in range(4)]

    def head_sq_norms(t):
        tf = t.astype(F32)
        sq = tf * tf
        return [jnp.sum(jnp.where(head_mask[h], sq, 0.0), axis=-1, keepdims=True)
                for h in range(4)]

    @pl.when(qi == 0)
    def _():
        kmax_ref[...] = jnp.zeros_like(kmax_ref)

        def scan(c, carry):
            norms = head_sq_norms(k_ref[pl.ds(pl.multiple_of(c * blk, blk), blk), :])
            for h in range(4):
                kmax_ref[h] = jnp.maximum(kmax_ref[h], jnp.max(norms[h], axis=0, keepdims=True))
            return carry

        lax.fori_loop(0, seq // blk, scan, 0)

    q = q_ref[...]
    q_heads = [jnp.where(head_mask[h], q, jnp.zeros_like(q)) for h in range(4)]
    q_norms = head_sq_norms(q)
    z_bound = [jnp.sqrt(q_norms[h] * kmax_ref[h]) * 1.001 + 1e-3 for h in range(4)]
    row = lax.broadcasted_iota(jnp.int32, (blk, blk), 0)
    col = lax.broadcasted_iota(jnp.int32, (blk, blk), 1)
    below_diag = col < row
    suffix = jnp.where(row >= col, 1.0, 0.0).astype(BF16)

    acc_ref[...] = jnp.zeros_like(acc_ref)
    run_ref[...] = jnp.zeros_like(run_ref)

    def cond(carry):
        j, live = carry
        return jnp.logical_and(j <= qi, live)

    def body(carry):
        j, _ = carry
        kb = qi - j
        start = pl.multiple_of(kb * blk, blk)
        k_blk = k_ref[pl.ds(start, blk), :]
        v_blk = v_ref[pl.ds(start, blk), :]
        valid = jnp.logical_or(below_diag, j > 0)
        weights = []
        slack = None
        for h in range(4):
            z = _nt_dot(q_heads[h], k_blk)
            sp = jnp.where(valid, _softplus(z), 0.0)
            hi, lo = _split2(sp)
            cs = _dot(hi, suffix) + _dot(lo, suffix)
            run = run_ref[h]
            w = jnp.exp(jnp.where(valid, z - cs - run, NEG_BIG))
            run = run + cs[:, 0:1]
            run_ref[h] = run
            weights.append(w.astype(BF16))
            head_slack = jnp.max(z_bound[h] - run)
            slack = head_slack if slack is None else jnp.maximum(slack, head_slack)
        wcat = jnp.concatenate(weights, axis=1)
        vcat = jnp.concatenate(
            [jnp.where(head_mask[h], v_blk, jnp.zeros_like(v_blk)) for h in range(4)], axis=0)
        acc_ref[...] += _dot(wcat, vcat)
        return j + 1, slack > EXP_UNDERFLOW

    lax.while_loop(cond, body, (jnp.int32(0), jnp.bool_(True)))
    o_ref[...] = acc_ref[...].astype(o_ref.dtype)


def _stick_breaking(h, bsz, seq, blk):
    n = bsz * seq
    nq = seq // blk
    cq, ck, cv = (COL_A // MIX_W + i for i in range(3))
    return pl.pallas_call(
        functools.partial(_sb_kernel, blk=blk, seq=seq),
        grid=(bsz, nq),
        in_specs=[pl.BlockSpec((blk, MIX_W), lambda b, i: (b * nq + i, cq)),
                  pl.BlockSpec((seq, MIX_W), lambda b, i: (b, ck)),
                  pl.BlockSpec((seq, MIX_W), lambda b, i: (b, cv))],
        out_specs=pl.BlockSpec((blk, MIX_W), lambda b, i: (b * nq + i, 0)),
        out_shape=jax.ShapeDtypeStruct((n, MIX_W), BF16),
        scratch_shapes=[pltpu.VMEM((blk, MIX_W), F32), pltpu.VMEM((4, blk, 1), F32),
                        pltpu.VMEM((4, 1, 1), F32)],
        compiler_params=_cparams("parallel", "arbitrary"),
        name="stick_breaking",
    )(h, h, h)


def _gelu_tanh(x):
    return 0.5 * x * (1.0 + jnp.tanh(0.7978845608028654 * (x + 0.044715 * (x * x * x))))


def _sg_kernel(u_ref, v_ref, g_ref, b_ref, w_ref, bias_ref, o_ref, *, tb):
    u = _gelu_tanh(u_ref[...].astype(F32))
    v = _gelu_tanh(v_ref[...].astype(F32))
    vn = _layer_norm(v, g_ref[...], b_ref[...]).astype(BF16)
    lane = lax.broadcasted_iota(jnp.int32, (1, MIX_W), 1)
    group_mask = [(lane // HEAD_DIM) == g for g in range(4)]
    row = lax.broadcasted_iota(jnp.int32, (SG_CHUNK, SG_CHUNK), 0)
    col = lax.broadcasted_iota(jnp.int32, (SG_CHUNK, SG_CHUNK), 1)
    causal = col <= row
    wcat = jnp.concatenate(
        [jnp.where(causal, w_ref[g], 0.0).astype(BF16) for g in range(4)], axis=1)
    bias = bias_ref[...]
    for c in range(tb // SG_CHUNK):
        sl = slice(c * SG_CHUNK, (c + 1) * SG_CHUNK)
        vc = vn[sl, :]
        vstack = jnp.concatenate(
            [jnp.where(group_mask[g], vc, jnp.zeros_like(vc)) for g in range(4)], axis=0)
        mixed = _dot(wcat, vstack) + bias
        o_ref[sl, :] = (u[sl, :] * mixed).astype(o_ref.dtype)


def _spatial_gating(h, ln_g, ln_b, w_s, bias_full, tb):
    n = h.shape[0]
    cu, cv = COL_B // MIX_W, COL_B // MIX_W + 1
    return pl.pallas_call(
        functools.partial(_sg_kernel, tb=tb),
        grid=(n // tb,),
        in_specs=[pl.BlockSpec((tb, MIX_W), lambda i: (i, cu)),
                  pl.BlockSpec((tb, MIX_W), lambda i: (i, cv)),
                  pl.BlockSpec((1, MIX_W), lambda i: (0, 0)),
                  pl.BlockSpec((1, MIX_W), lambda i: (0, 0)),
                  pl.BlockSpec((4, SG_CHUNK, SG_CHUNK), lambda i: (0, 0, 0)),
                  pl.BlockSpec((SG_CHUNK, MIX_W), lambda i: (0, 0))],
        out_specs=pl.BlockSpec((tb, MIX_W), lambda i: (i, 0)),
        out_shape=jax.ShapeDtypeStruct((n, MIX_W), BF16),
        compiler_params=_cparams("parallel"),
        name="spatial_gating",
    )(h, h, ln_g, ln_b, w_s, bias_full)


BAND_TQ = 256
BAND_PREV = BAND_LEFT * BAND_CHUNK
BAND_WIN = BAND_PREV + BAND_TQ


def _band_kernel(q_ref, k2_ref, k1_ref, k0_ref, v2_ref, v1_ref, v0_ref, bias_ref, o_ref, *, nblk):
    bi = pl.program_id(0) % nblk
    lane = lax.broadcasted_iota(jnp.int32, (1, MIX_W), 1)
    q = q_ref[...]
    kcat = jnp.concatenate([k2_ref[...], k1_ref[...], k0_ref[...]], axis=0)
    vcat = jnp.concatenate([v2_ref[...], v1_ref[...], v0_ref[...]], axis=0)
    col = lax.broadcasted_iota(jnp.int32, (1, BAND_WIN), 1)
    in_seq = col >= (2 - jnp.minimum(bi, 2)) * BAND_TQ
    out = jnp.zeros((BAND_TQ, MIX_W), F32)
    for h in range(4):
        hm = (lane // HEAD_DIM) == h
        s = _nt_dot(jnp.where(hm, q, jnp.zeros_like(q)), kcat) + bias_ref[h]
        s = jnp.where(in_seq, s, NEG_BIG)
        m = jnp.max(s, axis=-1, keepdims=True)
        p = jnp.exp(s - m)
        l = jnp.sum(p, axis=-1, keepdims=True)
        o = _dot(p.astype(BF16), vcat) / l
        out = jnp.where(hm, o, out)
    o_ref[...] = out.astype(o_ref.dtype)


def _band_attention(h, bias_full, seq):
    n = h.shape[0]
    nblk = seq // BAND_TQ
    cq, ck, cv = (COL_C // MIX_W + i for i in range(3))

    def prev(i, d):
        return i - jnp.minimum(i % nblk, d)

    def spec(c, d):
        return pl.BlockSpec((BAND_TQ, MIX_W), lambda i: (prev(i, d), c))

    return pl.pallas_call(
        functools.partial(_band_kernel, nblk=nblk),
        grid=(n // BAND_TQ,),
        in_specs=[spec(cq, 0), spec(ck, 2), spec(ck, 1), spec(ck, 0),
                  spec(cv, 2), spec(cv, 1), spec(cv, 0),
                  pl.BlockSpec((4, BAND_TQ, BAND_WIN), lambda i: (0, 0, 0))],
        out_specs=pl.BlockSpec((BAND_TQ, MIX_W), lambda i: (i, 0)),
        out_shape=jax.ShapeDtypeStruct((n, MIX_W), BF16),
        compiler_params=_cparams("parallel"),
        name="band_attention",
    )(h, h, h, h, h, h, h, bias_full)


def _band_bias(rel_bias):
    period = 1024
    u = jnp.arange(period)
    rel = jnp.where(u <= BAND_WIN, BAND_PREV - u, BAND_PREV + period - u)
    row0 = rel_bias.astype(F32)[:, jnp.clip(rel, -(BAND_CHUNK - 1), BAND_REL_MAX) + (BAND_CHUNK - 1)]
    skew = jnp.tile(row0, (1, BAND_TQ))[:, :BAND_TQ * (period - 1)]
    bias = skew.reshape(4, BAND_TQ, period - 1)[:, :, :BAND_WIN]
    tc = (jnp.arange(BAND_TQ)[:, None] + BAND_PREV) // BAND_CHUNK
    sc = jnp.arange(BAND_WIN)[None, :] // BAND_CHUNK
    in_band = jnp.logical_and(sc <= tc, sc >= tc - BAND_LEFT)
    return jnp.where(in_band[None], bias, NEG_BIG)


def _ssd_kernel(z_ref, xs_ref, bc_ref, dt_ref, cwx_ref, cbx_ref, cwb_ref, cbb_ref, dtb_ref,
                alog_ref, dsk_ref, ng_ref, o_ref, xpad_ref, bpad_ref, state_ref, *, q):
    c = pl.program_id(1)

    @pl.when(c == 0)
    def _():
        xpad_ref[0:8, :] = jnp.zeros((8, SSD_INNER), F32)
        bpad_ref[0:8, :] = jnp.zeros((8, SSD_BC), F32)
        state_ref[...] = jnp.zeros_like(state_ref)

    xpad_ref[8:8 + q, :] = xs_ref[...].astype(F32)
    bpad_ref[8:8 + q, :] = bc_ref[...].astype(F32)

    def conv_silu(pad_ref, w_ref, b_ref):
        acc = b_ref[...]
        for k in range(4):
            acc = acc + w_ref[k:k + 1, :] * pad_ref[5 + k:5 + k + q, :]
        return _silu(acc)

    xs = conv_silu(xpad_ref, cwx_ref, cbx_ref)
    bc = conv_silu(bpad_ref, cwb_ref, cbb_ref)
    xpad_ref[0:8, :] = xpad_ref[q:q + 8, :]
    bpad_ref[0:8, :] = bpad_ref[q:q + 8, :]
    bm = bc[:, 0:128].astype(BF16)
    cm = bc[:, 128:256].astype(BF16)

    r128 = lax.broadcasted_iota(jnp.int32, (128, SSD_INNER), 0)
    c512 = lax.broadcasted_iota(jnp.int32, (128, SSD_INNER), 1)
    expand = jnp.where(c512 // HEAD_DIM == r128, 1.0, 0.0).astype(BF16)
    pick = jnp.where(c512 == r128 * HEAD_DIM, 1.0, 0.0).astype(BF16)
    dt = _softplus(_dot(dt_ref[...], expand) + dtb_ref[...])
    da = dt * (-jnp.exp(alog_ref[...]))
    row = lax.broadcasted_iota(jnp.int32, (q, q), 0)
    col = lax.broadcasted_iota(jnp.int32, (q, q), 1)
    causal = col <= row
    tri = jnp.where(causal, 1.0, 0.0).astype(BF16)
    da_hi, da_lo = _split2(da)
    acs = _dot(tri, da_hi) + _dot(tri, da_lo)
    a_hi, a_mid, a_lo = _split3(acs)
    acs_t = _nt_dot(pick, a_hi) + _nt_dot(pick, a_mid) + _nt_dot(pick, a_lo)
    xdt = xs * dt

    eye = jnp.where(lax.broadcasted_iota(jnp.int32, (128, 128), 0)
                    == lax.broadcasted_iota(jnp.int32, (128, 128), 1), 1.0, 0.0).astype(BF16)
    bm_t = _nt_dot(eye, bm).astype(BF16)
    lane128 = lax.broadcasted_iota(jnp.int32, (1, 128), 1)
    lane256 = lax.broadcasted_iota(jnp.int32, (1, 256), 1)

    y_groups = []
    for g in range(2):
        gm = (lane128 // HEAD_DIM) == g
        cb = _nt_dot(jnp.where(gm, cm, jnp.zeros_like(cm)), bm)
        xg = xdt[:, g * 256:(g + 1) * 256].astype(BF16)
        ms, xstack = [], []
        for hh in range(4):
            head = g * 4 + hh
            seg = acs[:, head * HEAD_DIM:head * HEAD_DIM + 1] - acs_t[head:head + 1, :]
            decay = jnp.exp(jnp.where(causal, seg, NEG_BIG))
            ms.append((cb * decay).astype(BF16))
            xstack.append(jnp.where((lane256 // HEAD_DIM) == hh, xg, jnp.zeros_like(xg)))
        y_groups.append(_dot(jnp.concatenate(ms, axis=1), jnp.concatenate(xstack, axis=0)))
    y_diag = jnp.concatenate(y_groups, axis=1)

    state = state_ref[...]
    y_off = _dot(cm, state.astype(BF16)) * jnp.exp(acs)
    acs_last = acs[q - 1:q, :]
    xw = (xdt * jnp.exp(acs_last - acs)).astype(BF16)
    keep = (lax.broadcasted_iota(jnp.int32, (128, SSD_INNER), 0) // HEAD_DIM
            == lax.broadcasted_iota(jnp.int32, (128, SSD_INNER), 1) // 256)
    state_ref[...] = jnp.where(keep, state * jnp.exp(acs_last) + _dot(bm_t, xw), 0.0)

    y = y_diag + y_off + xs * dsk_ref[...]
    y = y * _silu(z_ref[...].astype(F32))
    outs = []
    for g in range(2):
        yg = y[:, g * 256:(g + 1) * 256]
        outs.append(yg * lax.rsqrt(jnp.mean(yg * yg, axis=-1, keepdims=True) + LN_EPS))
    o_ref[...] = (jnp.concatenate(outs, axis=1) * ng_ref[...]).astype(o_ref.dtype)


def _ssd(h, conv_wx, conv_bx, conv_wb, conv_bb, dt_bias_e, a_log_e, d_e, norm_g, bsz, seq, q):
    n = bsz * seq
    nc = seq // q

    def tok(width, colblk):
        return pl.BlockSpec((q, width), lambda b, c: (b * nc + c, colblk))

    def const(shape):
        return pl.BlockSpec(shape, lambda b, c: (0, 0))

    return pl.pallas_call(
        functools.partial(_ssd_kernel, q=q),
        grid=(bsz, nc),
        in_specs=[tok(SSD_INNER, COL_Z // SSD_INNER), tok(SSD_INNER, COL_XS // SSD_INNER),
                  tok(SSD_BC, COL_BC // SSD_BC), tok(128, COL_DT // 128),
                  const((4, SSD_INNER)), const((1, SSD_INNER)),
                  const((4, SSD_BC)), const((1, SSD_BC)),
                  const((1, SSD_INNER)), const((1, SSD_INNER)), const((1, SSD_INNER)),
                  const((1, SSD_INNER))],
        out_specs=pl.BlockSpec((q, SSD_INNER), lambda b, c: (b * nc + c, 0)),
        out_shape=jax.ShapeDtypeStruct((n, SSD_INNER), BF16),
        scratch_shapes=[pltpu.VMEM((q + 8, SSD_INNER), F32), pltpu.VMEM((q + 8, SSD_BC), F32),
                        pltpu.VMEM((128, SSD_INNER), F32)],
        compiler_params=_cparams("parallel", "arbitrary"),
        name="ssd",
    )(h, h, h, h, conv_wx, conv_bx, conv_wb, conv_bb, dt_bias_e, a_log_e, d_e, norm_g)


def _merge_kernel(x_ref, g_ref, ya_ref, yb_ref, yc_ref, yd_ref, wa_ref, wb_ref, wc_ref, wd_ref,
                  wo_ref, lg_ref, lb_ref, o_ref):
    merged = None
    for i, (y_ref, w_ref) in enumerate(((ya_ref, wa_ref), (yb_ref, wb_ref), (yc_ref, wc_ref),
                                        (yd_ref, wd_ref))):
        gate = jax.nn.sigmoid(g_ref[:, i * D_MODEL:(i + 1) * D_MODEL].astype(F32))
        term = gate * _dot(y_ref[...], w_ref[...])
        merged = term if merged is None else merged + term
    o = DEEPNORM_ALPHA * x_ref[...] + _dot(merged.astype(BF16), wo_ref[...])
    o_ref[...] = _layer_norm(o, lg_ref[...], lb_ref[...])


def _merge(x2, h, ya, yb, yc, yd, wa, wb, wc, wd, wo, lg, lb, tm):
    n = x2.shape[0]

    def tok(width):
        return pl.BlockSpec((tm, width), lambda i: (i, 0))

    def const(shape):
        return pl.BlockSpec(shape, lambda i: (0, 0))

    return pl.pallas_call(
        _merge_kernel,
        grid=(n // tm,),
        in_specs=[tok(D_MODEL), tok(4 * D_MODEL), tok(MIX_W), tok(MIX_W), tok(MIX_W),
                  tok(SSD_INNER), const((MIX_W, D_MODEL)), const((MIX_W, D_MODEL)),
                  const((MIX_W, D_MODEL)), const((SSD_INNER, D_MODEL)),
                  const((D_MODEL, D_MODEL)), const((1, D_MODEL)), const((1, D_MODEL))],
        out_specs=tok(D_MODEL),
        out_shape=jax.ShapeDtypeStruct((n, D_MODEL), F32),
        compiler_params=_cparams("parallel"),
        name="merge_ln1",
    )(x2, h, ya, yb, yc, yd, wa, wb, wc, wd, wo, lg, lb)


META_E0, META_E1, META_RANK0, META_RANK1, META_G0, META_G1 = range(6)


def _router_kernel(x_ref, w_ref, meta_ref, cnt_ref, base_ref, *, tm):
    @pl.when(pl.program_id(0) == 0)
    def _():
        base_ref[...] = jnp.zeros_like(base_ref)

    logits = jnp.dot(x_ref[...], w_ref[...], preferred_element_type=F32,
                     precision=lax.Precision.HIGHEST)
    lane = lax.broadcasted_iota(jnp.int32, logits.shape, 1)
    lg = jnp.where(lane < N_EXPERTS, logits, -jnp.inf)
    m1 = jnp.max(lg, axis=-1, keepdims=True)
    i1 = jnp.min(jnp.where(lg == m1, lane, 128), axis=-1, keepdims=True)
    first = lane == i1
    lg2 = jnp.where(first, -jnp.inf, lg)
    m2 = jnp.max(lg2, axis=-1, keepdims=True)
    i2 = jnp.min(jnp.where(lg2 == m2, lane, 128), axis=-1, keepdims=True)
    second = lane == i2
    e2 = jnp.exp(m2 - m1)
    denom = 1.0 + e2
    sel = jnp.where(jnp.logical_or(first, second), 1.0, 0.0)
    row = lax.broadcasted_iota(jnp.int32, (tm, tm), 0)
    col = lax.broadcasted_iota(jnp.int32, (tm, tm), 1)
    strict_lower = jnp.where(col < row, 1.0, 0.0).astype(BF16)
    pos = base_ref[...] + _dot(strict_lower, sel.astype(BF16))
    rank0 = jnp.sum(jnp.where(first, pos, 0.0), axis=-1, keepdims=True)
    rank1 = jnp.sum(jnp.where(second, pos, 0.0), axis=-1, keepdims=True)
    base_ref[...] += jnp.sum(sel, axis=0, keepdims=True)
    cnt_ref[...] = jnp.broadcast_to(base_ref[...], cnt_ref.shape)
    fields = (i1.astype(F32), i2.astype(F32), rank0, rank1, 1.0 / denom, e2 / denom)
    meta = jnp.zeros(logits.shape, F32)
    for k, val in enumerate(fields):
        meta = jnp.where(lane == k, val, meta)
    meta_ref[...] = meta


def _router(x2, w_router_padded, tm):
    n = x2.shape[0]
    return pl.pallas_call(
        functools.partial(_router_kernel, tm=tm),
        grid=(n // tm,),
        in_specs=[pl.BlockSpec((tm, D_MODEL), lambda i: (i, 0)),
                  pl.BlockSpec((D_MODEL, 128), lambda i: (0, 0))],
        out_specs=[pl.BlockSpec((tm, 128), lambda i: (i, 0)),
                   pl.BlockSpec((8, 128), lambda i: (0, 0))],
        out_shape=[jax.ShapeDtypeStruct((n, 128), F32), jax.ShapeDtypeStruct((8, 128), F32)],
        scratch_shapes=[pltpu.VMEM((1, 128), F32)],
        compiler_params=_cparams("arbitrary"),
        name="router",
    )(x2, w_router_padded)


ROWCOPY_STEP = 2048


def _rowcopy_kernel(sidx_ref, didx_ref, src_ref, *rest, rows):
    dst_ref, sem = rest[-2], rest[-1]

    def row_copy(s, d):
        return pltpu.make_async_copy(src_ref.at[pl.ds(s, 1), :], dst_ref.at[pl.ds(d, 1), :], sem)

    def issue(r, carry):
        row_copy(sidx_ref[0, r], didx_ref[0, r]).start()
        return carry

    def wait(r, carry):
        row_copy(sidx_ref[0, r], didx_ref[0, r]).wait()
        return carry

    lax.fori_loop(0, rows, issue, 0)
    lax.fori_loop(0, rows, wait, 0)


def _rowcopy(src, src_idx, dst_idx, dst_rows, dst_init=None):
    total = src_idx.shape[0]
    steps = total // ROWCOPY_STEP
    sidx = src_idx.reshape(steps, 1, ROWCOPY_STEP)
    didx = dst_idx.reshape(steps, 1, ROWCOPY_STEP)
    idx_spec = pl.BlockSpec((None, 1, ROWCOPY_STEP), lambda i: (i, 0, 0), memory_space=pltpu.SMEM)
    any_spec = pl.BlockSpec(memory_space=pl.ANY)
    operands = [sidx, didx, src] + ([] if dst_init is None else [dst_init])
    return pl.pallas_call(
        functools.partial(_rowcopy_kernel, rows=ROWCOPY_STEP),
        grid=(steps,),
        in_specs=[idx_spec, idx_spec] + [any_spec] * (len(operands) - 2),
        out_specs=any_spec,
        out_shape=jax.ShapeDtypeStruct((dst_rows, src.shape[1]), src.dtype),
        scratch_shapes=[pltpu.SemaphoreType.DMA],
        input_output_aliases={} if dst_init is None else {3: 0},
        compiler_params=pltpu.CompilerParams(dimension_semantics=("arbitrary",),
                                             has_side_effects=True),
        name="rowcopy",
    )(*operands)


def _gmm_kernel(te_ref, xs_ref, wg_ref, wu_ref, wd_ref, o_ref, acc_ref, xb_ref, *, n_tiles):
    i = pl.program_id(0)
    f = pl.program_id(1)

    @pl.when(f == 0)
    def _():
        acc_ref[...] = jnp.zeros_like(acc_ref)
        xb_ref[...] = xs_ref[...].astype(BF16)

    @pl.when(i < te_ref[n_tiles])
    def _():
        xb = xb_ref[...]
        hid = _silu(_dot(xb, wg_ref[...])) * _dot(xb, wu_ref[...])
        acc_ref[...] += _dot(hid.astype(BF16), wd_ref[...])

    @pl.when(f == pl.num_programs(1) - 1)
    def _():
        o_ref[...] = acc_ref[...]


def _grouped_swiglu(tile_expert, xs, wg, wu, wd, tm, tf):
    m = xs.shape[0]
    d_ff = wg.shape[2]
    n_tiles = m // tm
    grid_spec = pltpu.PrefetchScalarGridSpec(
        num_scalar_prefetch=1,
        grid=(n_tiles, d_ff // tf),
        in_specs=[pl.BlockSpec((tm, D_MODEL), lambda i, f, te: (i, 0)),
                  pl.BlockSpec((None, D_MODEL, tf), lambda i, f, te: (te[i], 0, f)),
                  pl.BlockSpec((None, D_MODEL, tf), lambda i, f, te: (te[i], 0, f)),
                  pl.BlockSpec((None, tf, D_MODEL), lambda i, f, te: (te[i], f, 0))],
        out_specs=pl.BlockSpec((tm, D_MODEL), lambda i, f, te: (i, 0)),
        scratch_shapes=[pltpu.VMEM((tm, D_MODEL), F32), pltpu.VMEM((tm, D_MODEL), BF16)])
    return pl.pallas_call(
        functools.partial(_gmm_kernel, n_tiles=n_tiles),
        grid_spec=grid_spec,
        out_shape=jax.ShapeDtypeStruct((m, D_MODEL), F32),
        compiler_params=_cparams("parallel", "arbitrary"),
        name="grouped_swiglu",
    )(tile_expert, xs, wg, wu, wd)


def _combine_kernel(x_ref, y0_ref, y1_ref, meta_ref, p_ref, pg_ref, pp_ref, lg_ref, lb_ref, o_ref):
    x = x_ref[...]
    meta = meta_ref[...]
    g0 = meta[:, META_G0:META_G0 + 1]
    g1 = meta[:, META_G1:META_G1 + 1]
    ple = (jax.nn.sigmoid(_dot(x.astype(BF16), pg_ref[...]))
           * _dot(p_ref[...].astype(BF16), pp_ref[...]))
    o = DEEPNORM_ALPHA * x + g0 * y0_ref[...] + g1 * y1_ref[...] + ple
    o_ref[...] = _layer_norm(o, lg_ref[...], lb_ref[...])


def _combine_ln2(x2, gathered, meta, p2, pg, pp, lg, lb, tm):
    n = x2.shape[0]
    ple_dim = p2.shape[1]
    nb = n // tm

    def tok(width):
        return pl.BlockSpec((tm, width), lambda i: (i, 0))

    def const(shape):
        return pl.BlockSpec(shape, lambda i: (0, 0))

    return pl.pallas_call(
        _combine_kernel,
        grid=(nb,),
        in_specs=[tok(D_MODEL), tok(D_MODEL), pl.BlockSpec((tm, D_MODEL), lambda i: (i + nb, 0)),
                  tok(128), tok(ple_dim), const((D_MODEL, D_MODEL)), const((ple_dim, D_MODEL)),
                  const((1, D_MODEL)), const((1, D_MODEL))],
        out_specs=tok(D_MODEL),
        out_shape=jax.ShapeDtypeStruct((n, D_MODEL), F32),
        compiler_params=_cparams("parallel"),
        name="combine_ple_ln2",
    )(x2, gathered, gathered, meta, p2, pg, pp, lg, lb)


def _moe(x2, w_router, wg, wu, wd, p2, pg, pp, lg, lb, tm, tf):
    n = x2.shape[0]
    wr = jnp.pad(w_router, ((0, 0), (0, 128 - N_EXPERTS)))
    meta, counts = _router(x2, wr, tm)
    cnt = counts[0, :N_EXPERTS].astype(jnp.int32)
    padded = ((cnt + tm - 1) // tm) * tm
    ends = jnp.cumsum(padded)
    starts = ends - padded
    experts = jnp.arange(N_EXPERTS, dtype=jnp.int32)

    def dest(e_lane, rank_lane):
        e = meta[:, e_lane].astype(jnp.int32)
        start = jnp.sum(jnp.where(e[:, None] == experts[None, :], starts[None, :], 0), axis=1)
        return start + meta[:, rank_lane].astype(jnp.int32)

    dest01 = jnp.concatenate([dest(META_E0, META_RANK0), dest(META_E1, META_RANK1)])
    m = 2 * n + N_EXPERTS * tm
    n_tiles = m // tm
    tile_start = jnp.arange(n_tiles, dtype=jnp.int32) * tm
    tile_expert = jnp.minimum(
        jnp.sum((ends[None, :] <= tile_start[:, None]).astype(jnp.int32), axis=1), N_EXPERTS - 1)
    prefetch = jnp.concatenate([tile_expert, (ends[-1] // tm)[None]]).astype(jnp.int32)
    tok = jnp.arange(n, dtype=jnp.int32)
    tok01 = jnp.concatenate([tok, tok])
    xs = _rowcopy(x2, tok01, dest01, m, dst_init=jnp.zeros((m, D_MODEL), F32))
    ys = _grouped_swiglu(prefetch, xs, wg, wu, wd, tm, tf)
    gathered = _rowcopy(ys, dest01, jnp.arange(2 * n, dtype=jnp.int32), 2 * n)
    return _combine_ln2(x2, gathered, meta, p2, pg, pp, lg, lb, tm)


def _ffn_kernel(x_ref, wg_ref, wu_ref, wd_ref, p_ref, pg_ref, pp_ref, lg_ref, lb_ref,
                o_ref, acc_ref, xb_ref):
    f = pl.program_id(1)

    @pl.when(f == 0)
    def _():
        acc_ref[...] = jnp.zeros_like(acc_ref)
        xb_ref[...] = x_ref[...].astype(BF16)

    xb = xb_ref[...]
    hid = _silu(_dot(xb, wg_ref[...])) * _dot(xb, wu_ref[...])
    acc_ref[...] += _dot(hid.astype(BF16), wd_ref[...])

    @pl.when(f == pl.num_programs(1) - 1)
    def _():
        ple = (jax.nn.sigmoid(_dot(xb, pg_ref[...]))
               * _dot(p_ref[...].astype(BF16), pp_ref[...]))
        o = DEEPNORM_ALPHA * x_ref[...] + acc_ref[...] + ple
        o_ref[...] = _layer_norm(o, lg_ref[...], lb_ref[...])


def _ffn(x2, wg, wu, wd, p2, pg, pp, lg, lb, tm, tf):
    n = x2.shape[0]
    d_ff = wg.shape[1]
    ple_dim = p2.shape[1]

    def tok(width):
        return pl.BlockSpec((tm, width), lambda i, f: (i, 0))

    def const(shape):
        return pl.BlockSpec(shape, lambda i, f: (0, 0))

    return pl.pallas_call(
        _ffn_kernel,
        grid=(n // tm, d_ff // tf),
        in_specs=[tok(D_MODEL),
                  pl.BlockSpec((D_MODEL, tf), lambda i, f: (0, f)),
                  pl.BlockSpec((D_MODEL, tf), lambda i, f: (0, f)),
                  pl.BlockSpec((tf, D_MODEL), lambda i, f: (f, 0)),
                  tok(ple_dim), const((D_MODEL, D_MODEL)), const((ple_dim, D_MODEL)),
                  const((1, D_MODEL)), const((1, D_MODEL))],
        out_specs=tok(D_MODEL),
        out_shape=jax.ShapeDtypeStruct((n, D_MODEL), F32),
        scratch_shapes=[pltpu.VMEM((tm, D_MODEL), F32), pltpu.VMEM((tm, D_MODEL), BF16)],
        compiler_params=_cparams("parallel", "arbitrary"),
        name="ffn_ple_ln2",
    )(x2, wg, wu, wd, p2, pg, pp, lg, lb)


def _prep_w_in(w):
    a, b, c, d, g = jnp.split(w, [768, 1280, 2048, 3336], axis=1)
    z, xs, bc, dt = jnp.split(d, [512, 1024, 1280], axis=1)

    def scale_q(t):
        return jnp.concatenate([t[:, :MIX_W] * (HEAD_DIM ** -0.5), t[:, MIX_W:]], axis=1)

    pad = jnp.zeros((w.shape[0], IN_COLS - (COL_DT + 8)), w.dtype)
    return jnp.concatenate([g, scale_q(a), b, scale_q(c), z, xs, bc, dt, pad], axis=1).astype(BF16)


def _row(v):
    return v.reshape(1, -1).astype(F32)


def _per_head(v):
    return _row(jnp.repeat(v, HEAD_DIM))


def kernel(x, p, w_in, w_br_a, w_br_b, w_br_c, w_br_d, w_out, sg_ln_g, sg_ln_b, sg_w, sg_b,
           ca_rel_bias, ssd_conv_w, ssd_conv_b, ssd_dt_bias, ssd_a_log, ssd_d, ssd_norm_g,
           ln1_g, ln1_b, ffn_w_gate, ffn_w_up, ffn_w_down, moe_router, moe_w_gate, moe_w_up,
           moe_w_down, ple_w_gate, ple_w_proj, ln2_g, ln2_b):
    bsz, seq, _ = x.shape
    n = bsz * seq
    x2 = x.reshape(n, D_MODEL)
    tm_proj = min(1024, n)
    tm = min(512, n)
    sb_blk = min(256, seq)
    ssd_q = min(256, seq)
    sg_tb = min(1024, seq)

    for i in range(DEPTH):
        h = _inproj(x2, _prep_w_in(w_in[i]), tm_proj, 1536)
        ya = _stick_breaking(h, bsz, seq, sb_blk)
        sg_bias = jnp.repeat(jnp.transpose(sg_b[i]), HEAD_DIM, axis=1).astype(F32)
        yb = _spatial_gating(h, _row(sg_ln_g[i]), _row(sg_ln_b[i]), sg_w[i], sg_bias, sg_tb)
        yc = _band_attention(h, _band_bias(ca_rel_bias[i]), seq)
        cw, cb = ssd_conv_w[i], ssd_conv_b[i]
        yd = _ssd(h, cw[:, :SSD_INNER], _row(cb[:SSD_INNER]), cw[:, SSD_INNER:],
                  _row(cb[SSD_INNER:]), _per_head(ssd_dt_bias[i]), _per_head(ssd_a_log[i]),
                  _per_head(ssd_d[i]), _row(ssd_norm_g[i]), bsz, seq, ssd_q)
        x2 = _merge(x2, h, ya, yb, yc, yd, w_br_a[i].astype(BF16), w_br_b[i].astype(BF16),
                    w_br_c[i].astype(BF16), w_br_d[i].astype(BF16), w_out[i].astype(BF16),
                    _row(ln1_g[i]), _row(ln1_b[i]), tm)
        p2 = p[i].reshape(n, -1)
        pg, pp = ple_w_gate[i].astype(BF16), ple_w_proj[i].astype(BF16)
        j = i // 2
        if i % 2 == 0:
            x2 = _ffn(x2, ffn_w_gate[j].astype(BF16), ffn_w_up[j].astype(BF16),
                      ffn_w_down[j].astype(BF16), p2, pg, pp, _row(ln2_g[i]), _row(ln2_b[i]),
                      tm, 1408)
        else:
            x2 = _moe(x2, moe_router[j], moe_w_gate[j].astype(BF16), moe_w_up[j].astype(BF16),
                      moe_w_down[j].astype(BF16), p2, pg, pp, _row(ln2_g[i]), _row(ln2_b[i]),
                      tm, 896)
    return x2.reshape(bsz, seq, D_MODEL)
```

```python
import functools

import jax
import jax.numpy as jnp
from jax import lax
from jax.experimental import pallas as pl
from jax.experimental.pallas import tpu as pltpu
from jax.experimental.pallas import tpu_sc as plsc

F32 = jnp.float32
BF16 = jnp.bfloat16

D_MODEL = 1024
DEPTH = 2
LN_EPS = 1e-5
DEEPNORM_ALPHA = (2 * DEPTH) ** 0.25
HEAD_DIM = 64
MIX_W = 256
SG_CHUNK = 128
BAND_CHUNK = 64
BAND_LEFT = 8
BAND_REL_MAX = 256
SSD_INNER = 512
SSD_HEADS = 8
SSD_BC = 256
N_EXPERTS = 8
NEG_BIG = -1e30
EXP_UNDERFLOW = -110.0

COL_GATE = 0
COL_A = 4096
COL_B = 4864
COL_C = 5376
COL_Z = 6144
COL_XS = 6656
COL_BC = 7168
COL_DT = 7424
IN_COLS = 7680

VMEM_LIMIT = 56 * 1024 * 1024


def _cparams(*sem):
    return pltpu.CompilerParams(dimension_semantics=sem, vmem_limit_bytes=VMEM_LIMIT)


def _nt_dot(a, b):
    return lax.dot_general(a, b, (((1,), (1,)), ((), ())), preferred_element_type=F32)


def _dot(a, b):
    return jnp.dot(a, b, preferred_element_type=F32)


def _softplus(x):
    return jnp.maximum(x, 0.0) + jnp.log(1.0 + jnp.exp(-jnp.abs(x)))


def _silu(x):
    return x * jax.nn.sigmoid(x)


def _split2(x):
    hi = x.astype(BF16)
    lo = (x - hi.astype(F32)).astype(BF16)
    return hi, lo


def _split3(x):
    hi = x.astype(BF16)
    r = x - hi.astype(F32)
    mid = r.astype(BF16)
    lo = (r - mid.astype(F32)).astype(BF16)
    return hi, mid, lo


def _layer_norm(x, g, b):
    mu = jnp.mean(x, axis=-1, keepdims=True)
    xc = x - mu
    var = jnp.mean(xc * xc, axis=-1, keepdims=True)
    return xc * lax.rsqrt(var + LN_EPS) * g + b


def _inproj_kernel(x_ref, w_ref, o_ref):
    o_ref[...] = _dot(x_ref[...].astype(BF16), w_ref[...]).astype(o_ref.dtype)


def _inproj(x2, w, tm, tn):
    n, d = x2.shape
    nc = w.shape[1]
    return pl.pallas_call(
        _inproj_kernel,
        grid=(n // tm, nc // tn),
        in_specs=[pl.BlockSpec((tm, d), lambda i, j: (i, 0)),
                  pl.BlockSpec((d, tn), lambda i, j: (0, j))],
        out_specs=pl.BlockSpec((tm, tn), lambda i, j: (i, j)),
        out_shape=jax.ShapeDtypeStruct((n, nc), BF16),
        compiler_params=_cparams("parallel", "arbitrary"),
        name="inproj",
    )(x2, w)


def _sb_kernel(q_ref, k_ref, v_ref, o_ref, acc_ref, run_ref, kmax_ref, *, blk, seq):
    qi = pl.program_id(1)
    lane = lax.broadcasted_iota(jnp.int32, (1, MIX_W), 1)
    head_mask = [(lane // HEAD_DIM) == h for h in range(4)]

    def head_sq_norms(t):
        tf = t.astype(F32)
        sq = tf * tf
        return [jnp.sum(jnp.where(head_mask[h], sq, 0.0), axis=-1, keepdims=True)
                for h in range(4)]

    @pl.when(qi == 0)
    def _():
        kmax_ref[...] = jnp.zeros_like(kmax_ref)

        def scan(c, carry):
            norms = head_sq_norms(k_ref[pl.ds(pl.multiple_of(c * blk, blk), blk), :])
            for h in range(4):
                kmax_ref[h] = jnp.maximum(kmax_ref[h], jnp.max(norms[h], axis=0, keepdims=True))
            return carry

        lax.fori_loop(0, seq // blk, scan, 0)

    q = q_ref[...]
    q_heads = [jnp.where(head_mask[h], q, jnp.zeros_like(q)) for h in range(4)]
    q_norms = head_sq_norms(q)
    z_bound = [jnp.sqrt(q_norms[h] * kmax_ref[h]) * 1.001 + 1e-3 for h in range(4)]
    row = lax.broadcasted_iota(jnp.int32, (blk, blk), 0)
    col = lax.broadcasted_iota(jnp.int32, (blk, blk), 1)
    below_diag = col < row
    suffix = jnp.where(row >= col, 1.0, 0.0).astype(BF16)

    acc_ref[...] = jnp.zeros_like(acc_ref)
    run_ref[...] = jnp.zeros_like(run_ref)

    def cond(carry):
        j, live = carry
        return jnp.logical_and(j <= qi, live)

    def body(carry):
        j, _ = carry
        kb = qi - j
        start = pl.multiple_of(kb * blk, blk)
        k_blk = k_ref[pl.ds(start, blk), :]
        v_blk = v_ref[pl.ds(start, blk), :]
        valid = jnp.logical_or(below_diag, j > 0)
        weights = []
        slack = None
        for h in range(4):
            z = _nt_dot(q_heads[h], k_blk)
            sp = jnp.where(valid, _softplus(z), 0.0)
            hi, lo = _split2(sp)
            cs = _dot(hi, suffix) + _dot(lo, suffix)
            run = run_ref[h]
            w = jnp.exp(jnp.where(valid, z - cs - run, NEG_BIG))
            run = run + cs[:, 0:1]
            run_ref[h] = run
            weights.append(w.astype(BF16))
            head_slack = jnp.max(z_bound[h] - run)
            slack = head_slack if slack is None else jnp.maximum(slack, head_slack)
        wcat = jnp.concatenate(weights, axis=1)
        vcat = jnp.concatenate(
            [jnp.where(head_mask[h], v_blk, jnp.zeros_like(v_blk)) for h in range(4)], axis=0)
        acc_ref[...] += _dot(wcat, vcat)
        return j + 1, slack > EXP_UNDERFLOW

    lax.while_loop(cond, body, (jnp.int32(0), jnp.bool_(True)))
    o_ref[...] = acc_ref[...].astype(o_ref.dtype)


def _stick_breaking(h, bsz, seq, blk):
    n = bsz * seq
    nq = seq // blk
    cq, ck, cv = (COL_A // MIX_W + i for i in range(3))
    return pl.pallas_call(
        functools.partial(_sb_kernel, blk=blk, seq=seq),
        grid=(bsz, nq),
        in_specs=[pl.BlockSpec((blk, MIX_W), lambda b, i: (b * nq + i, cq)),
                  pl.BlockSpec((seq, MIX_W), lambda b, i: (b, ck)),
                  pl.BlockSpec((seq, MIX_W), lambda b, i: (b, cv))],
        out_specs=pl.BlockSpec((blk, MIX_W), lambda b, i: (b * nq + i, 0)),
        out_shape=jax.ShapeDtypeStruct((n, MIX_W), BF16),
        scratch_shapes=[pltpu.VMEM((blk, MIX_W), F32), pltpu.VMEM((4, blk, 1), F32),
                        pltpu.VMEM((4, 1, 1), F32)],
        compiler_params=_cparams("parallel", "arbitrary"),
        name="stick_breaking",
    )(h, h, h)


def _gelu_tanh(x):
    return 0.5 * x * (1.0 + jnp.tanh(0.7978845608028654 * (x + 0.044715 * (x * x * x))))


def _sg_kernel(u_ref, v_ref, g_ref, b_ref, w_ref, bias_ref, o_ref, *, tb):
    u = _gelu_tanh(u_ref[...].astype(F32))
    v = _gelu_tanh(v_ref[...].astype(F32))
    vn = _layer_norm(v, g_ref[...], b_ref[...]).astype(BF16)
    lane = lax.broadcasted_iota(jnp.int32, (1, MIX_W), 1)
    group_mask = [(lane // HEAD_DIM) == g for g in range(4)]
    row = lax.broadcasted_iota(jnp.int32, (SG_CHUNK, SG_CHUNK), 0)
    col = lax.broadcasted_iota(jnp.int32, (SG_CHUNK, SG_CHUNK), 1)
    causal = col <= row
    wcat = jnp.concatenate(
        [jnp.where(causal, w_ref[g], 0.0).astype(BF16) for g in range(4)], axis=1)
    bias = bias_ref[...]
    for c in range(tb // SG_CHUNK):
        sl = slice(c * SG_CHUNK, (c + 1) * SG_CHUNK)
        vc = vn[sl, :]
        vstack = jnp.concatenate(
            [jnp.where(group_mask[g], vc, jnp.zeros_like(vc)) for g in range(4)], axis=0)
        mixed = _dot(wcat, vstack) + bias
        o_ref[sl, :] = (u[sl, :] * mixed).astype(o_ref.dtype)


def _spatial_gating(h, ln_g, ln_b, w_s, bias_full, tb):
    n = h.shape[0]
    cu, cv = COL_B // MIX_W, COL_B // MIX_W + 1
    return pl.pallas_call(
        functools.partial(_sg_kernel, tb=tb),
        grid=(n // tb,),
        in_specs=[pl.BlockSpec((tb, MIX_W), lambda i: (i, cu)),
                  pl.BlockSpec((tb, MIX_W), lambda i: (i, cv)),
                  pl.BlockSpec((1, MIX_W), lambda i: (0, 0)),
                  pl.BlockSpec((1, MIX_W), lambda i: (0, 0)),
                  pl.BlockSpec((4, SG_CHUNK, SG_CHUNK), lambda i: (0, 0, 0)),
                  pl.BlockSpec((SG_CHUNK, MIX_W), lambda i: (0, 0))],
        out_specs=pl.BlockSpec((tb, MIX_W), lambda i: (i, 0)),
        out_shape=jax.ShapeDtypeStruct((n, MIX_W), BF16),
        compiler_params=_cparams("parallel"),
        name="spatial_gating",
    )(h, h, ln_g, ln_b, w_s, bias_full)


BAND_TQ = 256
BAND_PREV = BAND_LEFT * BAND_CHUNK
BAND_WIN = BAND_PREV + BAND_TQ


def _band_kernel(q_ref, k2_ref, k1_ref, k0_ref, v2_ref, v1_ref, v0_ref, bias_ref, o_ref, *, nblk):
    bi = pl.program_id(0) % nblk
    lane = lax.broadcasted_iota(jnp.int32, (1, MIX_W), 1)
    q = q_ref[...]
    kcat = jnp.concatenate([k2_ref[...], k1_ref[...], k0_ref[...]], axis=0)
    vcat = jnp.concatenate([v2_ref[...], v1_ref[...], v0_ref[...]], axis=0)
    col = lax.broadcasted_iota(jnp.int32, (1, BAND_WIN), 1)
    in_seq = col >= (2 - jnp.minimum(bi, 2)) * BAND_TQ
    out = jnp.zeros((BAND_TQ, MIX_W), F32)
    for h in range(4):
        hm = (lane // HEAD_DIM) == h
        s = _nt_dot(jnp.where(hm, q, jnp.zeros_like(q)), kcat) + bias_ref[h]
        s = jnp.where(in_seq, s, NEG_BIG)
        m = jnp.max(s, axis=-1, keepdims=True)
        p = jnp.exp(s - m)
        l = jnp.sum(p, axis=-1, keepdims=True)
        o = _dot(p.astype(BF16), vcat) / l
        out = jnp.where(hm, o, out)
    o_ref[...] = out.astype(o_ref.dtype)


def _band_attention(h, bias_full, seq):
    n = h.shape[0]
    nblk = seq // BAND_TQ
    cq, ck, cv = (COL_C // MIX_W + i for i in range(3))

    def prev(i, d):
        return i - jnp.minimum(i % nblk, d)

    def spec(c, d):
        return pl.BlockSpec((BAND_TQ, MIX_W), lambda i: (prev(i, d), c))

    return pl.pallas_call(
        functools.partial(_band_kernel, nblk=nblk),
        grid=(n // BAND_TQ,),
        in_specs=[spec(cq, 0), spec(ck, 2), spec(ck, 1), spec(ck, 0),
                  spec(cv, 2), spec(cv, 1), spec(cv, 0),
                  pl.BlockSpec((4, BAND_TQ, BAND_WIN), lambda i: (0, 0, 0))],
        out_specs=pl.BlockSpec((BAND_TQ, MIX_W), lambda i: (i, 0)),
        out_shape=jax.ShapeDtypeStruct((n, MIX_W), BF16),
        compiler_params=_cparams("parallel"),
        name="band_attention",
    )(h, h, h, h, h, h, h, bias_full)


def _band_bias(rel_bias):
    period = 1024
    u = jnp.arange(period)
    rel = jnp.where(u <= BAND_WIN, BAND_PREV - u, BAND_PREV + period - u)
    row0 = rel_bias.astype(F32)[:, jnp.clip(rel, -(BAND_CHUNK - 1), BAND_REL_MAX) + (BAND_CHUNK - 1)]
    skew = jnp.tile(row0, (1, BAND_TQ))[:, :BAND_TQ * (period - 1)]
    bias = skew.reshape(4, BAND_TQ, period - 1)[:, :, :BAND_WIN]
    tc = (jnp.arange(BAND_TQ)[:, None] + BAND_PREV) // BAND_CHUNK
    sc = jnp.arange(BAND_WIN)[None, :] // BAND_CHUNK
    in_band = jnp.logical_and(sc <= tc, sc >= tc - BAND_LEFT)
    return jnp.where(in_band[None], bias, NEG_BIG)


def _ssd_kernel(z_ref, xs_ref, bc_ref, dt_ref, cwx_ref, cbx_ref, cwb_ref, cbb_ref, dtb_ref,
                alog_ref, dsk_ref, ng_ref, o_ref, xpad_ref, bpad_ref, state_ref, *, q):
    c = pl.program_id(1)

    @pl.when(c == 0)
    def _():
        xpad_ref[0:8, :] = jnp.zeros((8, SSD_INNER), F32)
        bpad_ref[0:8, :] = jnp.zeros((8, SSD_BC), F32)
        state_ref[...] = jnp.zeros_like(state_ref)

    xpad_ref[8:8 + q, :] = xs_ref[...].astype(F32)
    bpad_ref[8:8 + q, :] = bc_ref[...].astype(F32)

    def conv_silu(pad_ref, w_ref, b_ref):
        acc = b_ref[...]
        for k in range(4):
            acc = acc + w_ref[k:k + 1, :] * pad_ref[5 + k:5 + k + q, :]
        return _silu(acc)

    xs = conv_silu(xpad_ref, cwx_ref, cbx_ref)
    bc = conv_silu(bpad_ref, cwb_ref, cbb_ref)
    xpad_ref[0:8, :] = xpad_ref[q:q + 8, :]
    bpad_ref[0:8, :] = bpad_ref[q:q + 8, :]
    bm = bc[:, 0:128].astype(BF16)
    cm = bc[:, 128:256].astype(BF16)

    r128 = lax.broadcasted_iota(jnp.int32, (128, SSD_INNER), 0)
    c512 = lax.broadcasted_iota(jnp.int32, (128, SSD_INNER), 1)
    expand = jnp.where(c512 // HEAD_DIM == r128, 1.0, 0.0).astype(BF16)
    pick = jnp.where(c512 == r128 * HEAD_DIM, 1.0, 0.0).astype(BF16)
    dt = _softplus(_dot(dt_ref[...], expand) + dtb_ref[...])
    da = dt * (-jnp.exp(alog_ref[...]))
    row = lax.broadcasted_iota(jnp.int32, (q, q), 0)
    col = lax.broadcasted_iota(jnp.int32, (q, q), 1)
    causal = col <= row
    tri = jnp.where(causal, 1.0, 0.0).astype(BF16)
    da_hi, da_lo = _split2(da)
    acs = _dot(tri, da_hi) + _dot(tri, da_lo)
    a_hi, a_mid, a_lo = _split3(acs)
    acs_t = _nt_dot(pick, a_hi) + _nt_dot(pick, a_mid) + _nt_dot(pick, a_lo)
    xdt = xs * dt

    eye = jnp.where(lax.broadcasted_iota(jnp.int32, (128, 128), 0)
                    == lax.broadcasted_iota(jnp.int32, (128, 128), 1), 1.0, 0.0).astype(BF16)
    bm_t = _nt_dot(eye, bm).astype(BF16)
    lane128 = lax.broadcasted_iota(jnp.int32, (1, 128), 1)
    lane256 = lax.broadcasted_iota(jnp.int32, (1, 256), 1)

    y_groups = []
    for g in range(2):
        gm = (lane128 // HEAD_DIM) == g
        cb = _nt_dot(jnp.where(gm, cm, jnp.zeros_like(cm)), bm)
        xg = xdt[:, g * 256:(g + 1) * 256].astype(BF16)
        ms, xstack = [], []
        for hh in range(4):
            head = g * 4 + hh
            seg = acs[:, head * HEAD_DIM:head * HEAD_DIM + 1] - acs_t[head:head + 1, :]
            decay = jnp.exp(jnp.where(causal, seg, NEG_BIG))
            ms.append((cb * decay).astype(BF16))
            xstack.append(jnp.where((lane256 // HEAD_DIM) == hh, xg, jnp.zeros_like(xg)))
        y_groups.append(_dot(jnp.concatenate(ms, axis=1), jnp.concatenate(xstack, axis=0)))
    y_diag = jnp.concatenate(y_groups, axis=1)

    state = state_ref[...]
    y_off = _dot(cm, state.astype(BF16)) * jnp.exp(acs)
    acs_last = acs[q - 1:q, :]
    xw = (xdt * jnp.exp(acs_last - acs)).astype(BF16)
    keep = (lax.broadcasted_iota(jnp.int32, (128, SSD_INNER), 0) // HEAD_DIM
            == lax.broadcasted_iota(jnp.int32, (128, SSD_INNER), 1) // 256)
    state_ref[...] = jnp.where(keep, state * jnp.exp(acs_last) + _dot(bm_t, xw), 0.0)

    y = y_diag + y_off + xs * dsk_ref[...]
    y = y * _silu(z_ref[...].astype(F32))
    outs = []
    for g in range(2):
        yg = y[:, g * 256:(g + 1) * 256]
        outs.append(yg * lax.rsqrt(jnp.mean(yg * yg, axis=-1, keepdims=True) + LN_EPS))
    o_ref[...] = (jnp.concatenate(outs, axis=1) * ng_ref[...]).astype(o_ref.dtype)


def _ssd(h, conv_wx, conv_bx, conv_wb, conv_bb, dt_bias_e, a_log_e, d_e, norm_g, bsz, seq, q):
    n = bsz * seq
    nc = seq // q

    def tok(width, colblk):
        return pl.BlockSpec((q, width), lambda b, c: (b * nc + c, colblk))

    def const(shape):
        return pl.BlockSpec(shape, lambda b, c: (0, 0))

    return pl.pallas_call(
        functools.partial(_ssd_kernel, q=q),
        grid=(bsz, nc),
        in_specs=[tok(SSD_INNER, COL_Z // SSD_INNER), tok(SSD_INNER, COL_XS // SSD_INNER),
                  tok(SSD_BC, COL_BC // SSD_BC), tok(128, COL_DT // 128),
                  const((4, SSD_INNER)), const((1, SSD_INNER)),
                  const((4, SSD_BC)), const((1, SSD_BC)),
                  const((1, SSD_INNER)), const((1, SSD_INNER)), const((1, SSD_INNER)),
                  const((1, SSD_INNER))],
        out_specs=pl.BlockSpec((q, SSD_INNER), lambda b, c: (b * nc + c, 0)),
        out_shape=jax.ShapeDtypeStruct((n, SSD_INNER), BF16),
        scratch_shapes=[pltpu.VMEM((q + 8, SSD_INNER), F32), pltpu.VMEM((q + 8, SSD_BC), F32),
                        pltpu.VMEM((128, SSD_INNER), F32)],
        compiler_params=_cparams("parallel", "arbitrary"),
        name="ssd",
    )(h, h, h, h, conv_wx, conv_bx, conv_wb, conv_bb, dt_bias_e, a_log_e, d_e, norm_g)


def _merge_kernel(x_ref, g_ref, ya_ref, yb_ref, yc_ref, yd_ref, wa_ref, wb_ref, wc_ref, wd_ref,
                  wo_ref, lg_ref, lb_ref, o_ref):
    merged = None
    for i, (y_ref, w_ref) in enumerate(((ya_ref, wa_ref), (yb_ref, wb_ref), (yc_ref, wc_ref),
                                        (yd_ref, wd_ref))):
        gate = jax.nn.sigmoid(g_ref[:, i * D_MODEL:(i + 1) * D_MODEL].astype(F32))
        term = gate * _dot(y_ref[...], w_ref[...])
        merged = term if merged is None else merged + term
    o = DEEPNORM_ALPHA * x_ref[...] + _dot(merged.astype(BF16), wo_ref[...])
    o_ref[...] = _layer_norm(o, lg_ref[...], lb_ref[...])


def _merge(x2, h, ya, yb, yc, yd, wa, wb, wc, wd, wo, lg, lb, tm):
    n = x2.shape[0]

    def tok(width):
        return pl.BlockSpec((tm, width), lambda i: (i, 0))

    def const(shape):
        return pl.BlockSpec(shape, lambda i: (0, 0))

    return pl.pallas_call(
        _merge_kernel,
        grid=(n // tm,),
        in_specs=[tok(D_MODEL), tok(4 * D_MODEL), tok(MIX_W), tok(MIX_W), tok(MIX_W),
                  tok(SSD_INNER), const((MIX_W, D_MODEL)), const((MIX_W, D_MODEL)),
                  const((MIX_W, D_MODEL)), const((SSD_INNER, D_MODEL)),
                  const((D_MODEL, D_MODEL)), const((1, D_MODEL)), const((1, D_MODEL))],
        out_specs=tok(D_MODEL),
        out_shape=jax.ShapeDtypeStruct((n, D_MODEL), F32),
        compiler_params=_cparams("parallel"),
        name="merge_ln1",
    )(x2, h, ya, yb, yc, yd, wa, wb, wc, wd, wo, lg, lb)


META_E0, META_E1, META_RANK0, META_RANK1, META_G0, META_G1 = range(6)

SLAB = (8, 128)


def _to_slabs(x, slab_ref):
    for s in range(SLAB[0]):
        slab_ref[:, s, :] = x[:, s * 128:(s + 1) * 128]


def _from_slabs(slab_ref):
    return jnp.concatenate([slab_ref[:, s, :] for s in range(SLAB[0])], axis=1)


def _router_kernel(x_ref, w_ref, meta_ref, cnt_ref, slab_ref, base_ref, *, tm):
    @pl.when(pl.program_id(0) == 0)
    def _():
        base_ref[...] = jnp.zeros_like(base_ref)

    x = x_ref[...]
    _to_slabs(x, slab_ref)
    logits = jnp.dot(x, w_ref[...], preferred_element_type=F32,
                     precision=lax.Precision.HIGHEST)
    lane = lax.broadcasted_iota(jnp.int32, logits.shape, 1)
    lg = jnp.where(lane < N_EXPERTS, logits, -jnp.inf)
    m1 = jnp.max(lg, axis=-1, keepdims=True)
    i1 = jnp.min(jnp.where(lg == m1, lane, 128), axis=-1, keepdims=True)
    first = lane == i1
    lg2 = jnp.where(first, -jnp.inf, lg)
    m2 = jnp.max(lg2, axis=-1, keepdims=True)
    i2 = jnp.min(jnp.where(lg2 == m2, lane, 128), axis=-1, keepdims=True)
    second = lane == i2
    e2 = jnp.exp(m2 - m1)
    denom = 1.0 + e2
    sel = jnp.where(jnp.logical_or(first, second), 1.0, 0.0)
    row = lax.broadcasted_iota(jnp.int32, (tm, tm), 0)
    col = lax.broadcasted_iota(jnp.int32, (tm, tm), 1)
    strict_lower = jnp.where(col < row, 1.0, 0.0).astype(BF16)
    pos = base_ref[...] + _dot(strict_lower, sel.astype(BF16))
    rank0 = jnp.sum(jnp.where(first, pos, 0.0), axis=-1, keepdims=True)
    rank1 = jnp.sum(jnp.where(second, pos, 0.0), axis=-1, keepdims=True)
    base_ref[...] += jnp.sum(sel, axis=0, keepdims=True)
    cnt_ref[...] = jnp.broadcast_to(base_ref[...], cnt_ref.shape)
    fields = (i1.astype(F32), i2.astype(F32), rank0, rank1, 1.0 / denom, e2 / denom)
    meta = jnp.zeros(logits.shape, F32)
    for k, val in enumerate(fields):
        meta = jnp.where(lane == k, val, meta)
    meta_ref[...] = meta


def _router(x2, w_router_padded, tm):
    n = x2.shape[0]
    return pl.pallas_call(
        functools.partial(_router_kernel, tm=tm),
        grid=(n // tm,),
        in_specs=[pl.BlockSpec((tm, D_MODEL), lambda i: (i, 0)),
                  pl.BlockSpec((D_MODEL, 128), lambda i: (0, 0))],
        out_specs=[pl.BlockSpec((tm, 128), lambda i: (i, 0)),
                   pl.BlockSpec((8, 128), lambda i: (0, 0)),
                   pl.BlockSpec((tm,) + SLAB, lambda i: (i, 0, 0))],
        out_shape=[jax.ShapeDtypeStruct((n, 128), F32), jax.ShapeDtypeStruct((8, 128), F32),
                   jax.ShapeDtypeStruct((n,) + SLAB, F32)],
        scratch_shapes=[pltpu.VMEM((1, 128), F32)],
        compiler_params=_cparams("arbitrary"),
        name="router",
    )(x2, w_router_padded)


GATHER_WINDOW = 128


def _gather_rows(table, idx):
    n = idx.shape[0]
    mesh = plsc.VectorSubcoreMesh(core_axis_name="core", subcore_axis_name="subcore")

    @pl.kernel(out_type=jax.ShapeDtypeStruct((n, table.shape[1]), table.dtype), mesh=mesh)
    def gather(table_hbm, idx_hbm, out_hbm):
        def body(idx_vmem, out_vmem):
            pltpu.sync_copy(table_hbm.at[idx_vmem.at[0]], out_vmem)

        pltpu.emit_pipeline(
            body,
            grid=(n // GATHER_WINDOW,),
            in_specs=[pl.BlockSpec((1, GATHER_WINDOW), index_map=lambda i: (0, i))],
            out_specs=[pl.BlockSpec((GATHER_WINDOW, table.shape[1]), index_map=lambda i: (i, 0))],
            core_axis_name=("core", "subcore"),
            dimension_semantics=(pltpu.PARALLEL,),
        )(idx_hbm, out_hbm)

    return gather(table, idx.reshape(1, n))


def _gather_slabs(slabs, rows):
    sub = jnp.arange(SLAB[0], dtype=jnp.int32)
    idx = (rows[:, None] * SLAB[0] + sub[None, :]).reshape(-1)
    out = _gather_rows(slabs.reshape(-1, SLAB[1]), idx)
    return out.reshape((rows.shape[0],) + SLAB)


def _gmm_kernel(te_ref, xs_ref, wg_ref, wu_ref, wd_ref, o_ref, acc_ref, xb_ref, *, n_tiles):
    i = pl.program_id(0)
    f = pl.program_id(1)

    @pl.when(f == 0)
    def _():
        acc_ref[...] = jnp.zeros_like(acc_ref)
        xb_ref[...] = _from_slabs(xs_ref).astype(BF16)

    @pl.when(i < te_ref[n_tiles])
    def _():
        xb = xb_ref[...]
        hid = _silu(_dot(xb, wg_ref[...])) * _dot(xb, wu_ref[...])
        acc_ref[...] += _dot(hid.astype(BF16), wd_ref[...])

    @pl.when(f == pl.num_programs(1) - 1)
    def _():
        _to_slabs(acc_ref[...], o_ref)


def _grouped_swiglu(tile_expert, xs, wg, wu, wd, tm, tf):
    m = xs.shape[0]
    d_ff = wg.shape[2]
    n_tiles = m // tm
    grid_spec = pltpu.PrefetchScalarGridSpec(
        num_scalar_prefetch=1,
        grid=(n_tiles, d_ff // tf),
        in_specs=[pl.BlockSpec((tm,) + SLAB, lambda i, f, te: (i, 0, 0)),
                  pl.BlockSpec((None, D_MODEL, tf), lambda i, f, te: (te[i], 0, f)),
                  pl.BlockSpec((None, D_MODEL, tf), lambda i, f, te: (te[i], 0, f)),
                  pl.BlockSpec((None, tf, D_MODEL), lambda i, f, te: (te[i], f, 0))],
        out_specs=pl.BlockSpec((tm,) + SLAB, lambda i, f, te: (i, 0, 0)),
        scratch_shapes=[pltpu.VMEM((tm, D_MODEL), F32), pltpu.VMEM((tm, D_MODEL), BF16)])
    return pl.pallas_call(
        functools.partial(_gmm_kernel, n_tiles=n_tiles),
        grid_spec=grid_spec,
        out_shape=jax.ShapeDtypeStruct((m,) + SLAB, F32),
        compiler_params=_cparams("parallel", "arbitrary"),
        name="grouped_swiglu",
    )(tile_expert, xs, wg, wu, wd)


def _combine_kernel(x_ref, y0_ref, y1_ref, meta_ref, p_ref, pg_ref, pp_ref, lg_ref, lb_ref, o_ref):
    x = x_ref[...]
    meta = meta_ref[...]
    g0 = meta[:, META_G0:META_G0 + 1]
    g1 = meta[:, META_G1:META_G1 + 1]
    ple = (jax.nn.sigmoid(_dot(x.astype(BF16), pg_ref[...]))
           * _dot(p_ref[...].astype(BF16), pp_ref[...]))
    o = DEEPNORM_ALPHA * x + g0 * _from_slabs(y0_ref) + g1 * _from_slabs(y1_ref) + ple
    o_ref[...] = _layer_norm(o, lg_ref[...], lb_ref[...])


def _combine_ln2(x2, gathered, meta, p2, pg, pp, lg, lb, tm):
    n = x2.shape[0]
    ple_dim = p2.shape[1]
    nb = n // tm

    def tok(width):
        return pl.BlockSpec((tm, width), lambda i: (i, 0))

    def const(shape):
        return pl.BlockSpec(shape, lambda i: (0, 0))

    return pl.pallas_call(
        _combine_kernel,
        grid=(nb,),
        in_specs=[tok(D_MODEL), pl.BlockSpec((tm,) + SLAB, lambda i: (i, 0, 0)),
                  pl.BlockSpec((tm,) + SLAB, lambda i: (i + nb, 0, 0)),
                  tok(128), tok(ple_dim), const((D_MODEL, D_MODEL)), const((ple_dim, D_MODEL)),
                  const((1, D_MODEL)), const((1, D_MODEL))],
        out_specs=tok(D_MODEL),
        out_shape=jax.ShapeDtypeStruct((n, D_MODEL), F32),
        compiler_params=_cparams("parallel"),
        name="combine_ple_ln2",
    )(x2, gathered, gathered, meta, p2, pg, pp, lg, lb)


def _moe(x2, w_router, wg, wu, wd, p2, pg, pp, lg, lb, tm, tf):
    n = x2.shape[0]
    wr = jnp.pad(w_router, ((0, 0), (0, 128 - N_EXPERTS)))
    meta, counts, x_slabs = _router(x2, wr, tm)
    cnt = counts[0, :N_EXPERTS].astype(jnp.int32)
    padded = ((cnt + tm - 1) // tm) * tm
    ends = jnp.cumsum(padded)
    starts = ends - padded
    experts = jnp.arange(N_EXPERTS, dtype=jnp.int32)

    def dest(e_lane, rank_lane):
        e = meta[:, e_lane].astype(jnp.int32)
        start = jnp.sum(jnp.where(e[:, None] == experts[None, :], starts[None, :], 0), axis=1)
        return start + meta[:, rank_lane].astype(jnp.int32)

    dest01 = jnp.concatenate([dest(META_E0, META_RANK0), dest(META_E1, META_RANK1)])
    m = 2 * n + N_EXPERTS * tm
    n_tiles = m // tm
    tile_start = jnp.arange(n_tiles, dtype=jnp.int32) * tm
    tile_expert = jnp.minimum(
        jnp.sum((ends[None, :] <= tile_start[:, None]).astype(jnp.int32), axis=1), N_EXPERTS - 1)
    prefetch = jnp.concatenate([tile_expert, (ends[-1] // tm)[None]]).astype(jnp.int32)
    tok = jnp.arange(n, dtype=jnp.int32)
    src_tok = jnp.zeros((m,), jnp.int32).at[dest01].set(jnp.concatenate([tok, tok]),
                                                       unique_indices=True)
    xs = _gather_slabs(x_slabs, src_tok)
    ys = _grouped_swiglu(prefetch, xs, wg, wu, wd, tm, tf)
    gathered = _gather_slabs(ys, dest01)
    return _combine_ln2(x2, gathered, meta, p2, pg, pp, lg, lb, tm)


def _ffn_kernel(x_ref, wg_ref, wu_ref, wd_ref, p_ref, pg_ref, pp_ref, lg_ref, lb_ref,
                o_ref, acc_ref, xb_ref):
    f = pl.program_id(1)

    @pl.when(f == 0)
    def _():
        acc_ref[...] = jnp.zeros_like(acc_ref)
        xb_ref[...] = x_ref[...].astype(BF16)

    xb = xb_ref[...]
    hid = _silu(_dot(xb, wg_ref[...])) * _dot(xb, wu_ref[...])
    acc_ref[...] += _dot(hid.astype(BF16), wd_ref[...])

    @pl.when(f == pl.num_programs(1) - 1)
    def _():
        ple = (jax.nn.sigmoid(_dot(xb, pg_ref[...]))
               * _dot(p_ref[...].astype(BF16), pp_ref[...]))
        o = DEEPNORM_ALPHA * x_ref[...] + acc_ref[...] + ple
        o_ref[...] = _layer_norm(o, lg_ref[...], lb_ref[...])


def _ffn(x2, wg, wu, wd, p2, pg, pp, lg, lb, tm, tf):
    n = x2.shape[0]
    d_ff = wg.shape[1]
    ple_dim = p2.shape[1]

    def tok(width):
        return pl.BlockSpec((tm, width), lambda i, f: (i, 0))

    def const(shape):
        return pl.BlockSpec(shape, lambda i, f: (0, 0))

    return pl.pallas_call(
        _ffn_kernel,
        grid=(n // tm, d_ff // tf),
        in_specs=[tok(D_MODEL),
                  pl.BlockSpec((D_MODEL, tf), lambda i, f: (0, f)),
                  pl.BlockSpec((D_MODEL, tf), lambda i, f: (0, f)),
                  pl.BlockSpec((tf, D_MODEL), lambda i, f: (f, 0)),
                  tok(ple_dim), const((D_MODEL, D_MODEL)), const((ple_dim, D_MODEL)),
                  const((1, D_MODEL)), const((1, D_MODEL))],
        out_specs=tok(D_MODEL),
        out_shape=jax.ShapeDtypeStruct((n, D_MODEL), F32),
        scratch_shapes=[pltpu.VMEM((tm, D_MODEL), F32), pltpu.VMEM((tm, D_MODEL), BF16)],
        compiler_params=_cparams("parallel", "arbitrary"),
        name="ffn_ple_ln2",
    )(x2, wg, wu, wd, p2, pg, pp, lg, lb)


def _prep_w_in(w):
    a, b, c, d, g = jnp.split(w, [768, 1280, 2048, 3336], axis=1)
    z, xs, bc, dt = jnp.split(d, [512, 1024, 1280], axis=1)

    def scale_q(t):
        return jnp.concatenate([t[:, :MIX_W] * (HEAD_DIM ** -0.5), t[:, MIX_W:]], axis=1)

    pad = jnp.zeros((w.shape[0], IN_COLS - (COL_DT + 8)), w.dtype)
    return jnp.concatenate([g, scale_q(a), b, scale_q(c), z, xs, bc, dt, pad], axis=1).astype(BF16)


def _row(v):
    return v.reshape(1, -1).astype(F32)


def _per_head(v):
    return _row(jnp.repeat(v, HEAD_DIM))


def kernel(x, p, w_in, w_br_a, w_br_b, w_br_c, w_br_d, w_out, sg_ln_g, sg_ln_b, sg_w, sg_b,
           ca_rel_bias, ssd_conv_w, ssd_conv_b, ssd_dt_bias, ssd_a_log, ssd_d, ssd_norm_g,
           ln1_g, ln1_b, ffn_w_gate, ffn_w_up, ffn_w_down, moe_router, moe_w_gate, moe_w_up,
           moe_w_down, ple_w_gate, ple_w_proj, ln2_g, ln2_b):
    bsz, seq, _ = x.shape
    n = bsz * seq
    x2 = x.reshape(n, D_MODEL)
    tm_proj = min(1024, n)
    tm = min(512, n)
    sb_blk = min(256, seq)
    ssd_q = min(256, seq)
    sg_tb = min(1024, seq)

    for i in range(DEPTH):
        h = _inproj(x2, _prep_w_in(w_in[i]), tm_proj, 1536)
        ya = _stick_breaking(h, bsz, seq, sb_blk)
        sg_bias = jnp.repeat(jnp.transpose(sg_b[i]), HEAD_DIM, axis=1).astype(F32)
        yb = _spatial_gating(h, _row(sg_ln_g[i]), _row(sg_ln_b[i]), sg_w[i], sg_bias, sg_tb)
        yc = _band_attention(h, _band_bias(ca_rel_bias[i]), seq)
        cw, cb = ssd_conv_w[i], ssd_conv_b[i]
        yd = _ssd(h, cw[:, :SSD_INNER], _row(cb[:SSD_INNER]), cw[:, SSD_INNER:],
                  _row(cb[SSD_INNER:]), _per_head(ssd_dt_bias[i]), _per_head(ssd_a_log[i]),
                  _per_head(ssd_d[i]), _row(ssd_norm_g[i]), bsz, seq, ssd_q)
        x2 = _merge(x2, h, ya, yb, yc, yd, w_br_a[i].astype(BF16), w_br_b[i].astype(BF16),
                    w_br_c[i].astype(BF16), w_br_d[i].astype(BF16), w_out[i].astype(BF16),
                    _row(ln1_g[i]), _row(ln1_b[i]), tm)
        p2 = p[i].reshape(n, -1)
        pg, pp = ple_w_gate[i].astype(BF16), ple_w_proj[i].astype(BF16)
        j = i // 2
        if i % 2 == 0:
            x2 = _ffn(x2, ffn_w_gate[j].astype(BF16), ffn_w_up[j].astype(BF16),
                      ffn_w_down[j].astype(BF16), p2, pg, pp, _row(ln2_g[i]), _row(ln2_b[i]),
                      tm, 1408)
        else:
            x2 = _moe(x2, moe_router[j], moe_w_gate[j].astype(BF16), moe_w_up[j].astype(BF16),
                      moe_w_down[j].astype(BF16), p2, pg, pp, _row(ln2_g[i]), _row(ln2_b[i]),
                      tm, 896)
    return x2.reshape(bsz, seq, D_MODEL)
```

```python
import functools

import jax
import jax.numpy as jnp
from jax import lax
from jax.experimental import pallas as pl
from jax.experimental.pallas import tpu as pltpu
from jax.experimental.pallas import tpu_sc as plsc

F32 = jnp.float32
BF16 = jnp.bfloat16

D_MODEL = 1024
DEPTH = 2
LN_EPS = 1e-5
DEEPNORM_ALPHA = (2 * DEPTH) ** 0.25
HEAD_DIM = 64
MIX_W = 256
SG_CHUNK = 128
BAND_CHUNK = 64
BAND_LEFT = 8
BAND_REL_MAX = 256
SSD_INNER = 512
SSD_HEADS = 8
SSD_BC = 256
N_EXPERTS = 8
NEG_BIG = -1e30
LOG2_E = 1.4426950408889634
EXP2_UNDERFLOW = -160.0

COL_GATE = 0
COL_A = 4096
COL_B = 4864
COL_C = 5376
COL_Z = 6144
COL_XS = 6656
COL_BC = 7168
COL_DT = 7424
IN_COLS = 7680

VMEM_LIMIT = 56 * 1024 * 1024


def _cparams(*sem):
    return pltpu.CompilerParams(dimension_semantics=sem, vmem_limit_bytes=VMEM_LIMIT)


def _nt_dot(a, b):
    return lax.dot_general(a, b, (((1,), (1,)), ((), ())), preferred_element_type=F32)


def _dot(a, b):
    return jnp.dot(a, b, preferred_element_type=F32)


def _softplus(x):
    return jnp.maximum(x, 0.0) + jnp.log(1.0 + jnp.exp(-jnp.abs(x)))


def _softplus2(x):
    return jnp.maximum(x, 0.0) + jnp.log2(1.0 + jnp.exp2(-jnp.abs(x)))


def _silu(x):
    return x * jax.nn.sigmoid(x)


def _split2(x):
    hi = x.astype(BF16)
    lo = (x - hi.astype(F32)).astype(BF16)
    return hi, lo


def _split3(x):
    hi = x.astype(BF16)
    r = x - hi.astype(F32)
    mid = r.astype(BF16)
    lo = (r - mid.astype(F32)).astype(BF16)
    return hi, mid, lo


def _layer_norm(x, g, b):
    mu = jnp.mean(x, axis=-1, keepdims=True)
    xc = x - mu
    var = jnp.mean(xc * xc, axis=-1, keepdims=True)
    return xc * lax.rsqrt(var + LN_EPS) * g + b


def _inproj_kernel(x_ref, w_ref, o_ref):
    o_ref[...] = _dot(x_ref[...].astype(BF16), w_ref[...]).astype(o_ref.dtype)


def _inproj(x2, w, tm, tn):
    n, d = x2.shape
    nc = w.shape[1]
    return pl.pallas_call(
        _inproj_kernel,
        grid=(n // tm, nc // tn),
        in_specs=[pl.BlockSpec((tm, d), lambda i, j: (i, 0)),
                  pl.BlockSpec((d, tn), lambda i, j: (0, j))],
        out_specs=pl.BlockSpec((tm, tn), lambda i, j: (i, j)),
        out_shape=jax.ShapeDtypeStruct((n, nc), BF16),
        compiler_params=_cparams("parallel", "arbitrary"),
        name="inproj",
    )(x2, w)


def _sb_kernel(q_ref, k_ref, v_ref, o_ref, acc_ref, run_ref, kmax_ref, *, blk, seq):
    qi = pl.program_id(1)
    lane = lax.broadcasted_iota(jnp.int32, (1, MIX_W), 1)
    head_mask = [(lane // HEAD_DIM) == h for h in range(4)]

    def head_sq_norms(t):
        tf = t.astype(F32)
        sq = tf * tf
        return [jnp.sum(jnp.where(head_mask[h], sq, 0.0), axis=-1, keepdims=True)
                for h in range(4)]

    @pl.when(qi == 0)
    def _():
        kmax_ref[...] = jnp.zeros_like(kmax_ref)

        def scan(c, carry):
            norms = head_sq_norms(k_ref[pl.ds(pl.multiple_of(c * blk, blk), blk), :])
            for h in range(4):
                kmax_ref[h] = jnp.maximum(kmax_ref[h], jnp.max(norms[h], axis=0, keepdims=True))
            return carry

        lax.fori_loop(0, seq // blk, scan, 0)

    q = q_ref[...]
    q_heads = [jnp.where(head_mask[h], q, jnp.zeros_like(q)) for h in range(4)]
    q_norms = head_sq_norms(q)
    z_bound = [jnp.sqrt(q_norms[h] * kmax_ref[h]) * 1.001 + 1e-3 for h in range(4)]
    row = lax.broadcasted_iota(jnp.int32, (blk, blk), 0)
    col = lax.broadcasted_iota(jnp.int32, (blk, blk), 1)
    below_diag = col < row
    suffix = jnp.where(row >= col, 1.0, 0.0).astype(BF16)
    suffix2 = jnp.concatenate([suffix, suffix], axis=0)

    acc_ref[...] = jnp.zeros_like(acc_ref)
    run_ref[...] = jnp.zeros_like(run_ref)

    def process(j, masked):
        kb = qi - j
        start = pl.multiple_of(kb * blk, blk)
        k_blk = k_ref[pl.ds(start, blk), :]
        v_blk = v_ref[pl.ds(start, blk), :]
        weights = []
        slack = None
        for h in range(4):
            z = _nt_dot(q_heads[h], k_blk)
            sp = _softplus2(z)
            if masked:
                sp = jnp.where(below_diag, sp, 0.0)
            cs = _dot(jnp.concatenate(_split2(sp), axis=1), suffix2)
            run = run_ref[h]
            arg = z - cs - run
            if masked:
                arg = jnp.where(below_diag, arg, NEG_BIG)
            w = jnp.exp2(arg)
            run = run + cs[:, 0:1]
            run_ref[h] = run
            weights.append(w.astype(BF16))
            head_slack = jnp.max(z_bound[h] - run)
            slack = head_slack if slack is None else jnp.maximum(slack, head_slack)
        wcat = jnp.concatenate(weights, axis=1)
        vcat = jnp.concatenate(
            [jnp.where(head_mask[h], v_blk, jnp.zeros_like(v_blk)) for h in range(4)], axis=0)
        acc_ref[...] += _dot(wcat, vcat)
        return slack > EXP2_UNDERFLOW

    def cond(carry):
        j, live = carry
        return jnp.logical_and(j <= qi, live)

    def body(carry):
        j, _ = carry
        return j + 1, process(j, masked=False)

    lax.while_loop(cond, body, (jnp.int32(1), process(0, masked=True)))
    o_ref[...] = acc_ref[...].astype(o_ref.dtype)


def _stick_breaking(h, bsz, seq, blk):
    n = bsz * seq
    nq = seq // blk
    cq, ck, cv = (COL_A // MIX_W + i for i in range(3))
    return pl.pallas_call(
        functools.partial(_sb_kernel, blk=blk, seq=seq),
        grid=(bsz, nq),
        in_specs=[pl.BlockSpec((blk, MIX_W), lambda b, i: (b * nq + i, cq)),
                  pl.BlockSpec((seq, MIX_W), lambda b, i: (b, ck)),
                  pl.BlockSpec((seq, MIX_W), lambda b, i: (b, cv))],
        out_specs=pl.BlockSpec((blk, MIX_W), lambda b, i: (b * nq + i, 0)),
        out_shape=jax.ShapeDtypeStruct((n, MIX_W), BF16),
        scratch_shapes=[pltpu.VMEM((blk, MIX_W), F32), pltpu.VMEM((4, blk, 1), F32),
                        pltpu.VMEM((4, 1, 1), F32)],
        compiler_params=_cparams("parallel", "arbitrary"),
        name="stick_breaking",
    )(h, h, h)


def _gelu_tanh(x):
    return 0.5 * x * (1.0 + jnp.tanh(0.7978845608028654 * (x + 0.044715 * (x * x * x))))


def _sg_kernel(u_ref, v_ref, g_ref, b_ref, w_ref, bias_ref, o_ref, *, tb):
    u = _gelu_tanh(u_ref[...].astype(F32))
    v = _gelu_tanh(v_ref[...].astype(F32))
    vn = _layer_norm(v, g_ref[...], b_ref[...]).astype(BF16)
    lane = lax.broadcasted_iota(jnp.int32, (1, MIX_W), 1)
    group_mask = [(lane // HEAD_DIM) == g for g in range(4)]
    row = lax.broadcasted_iota(jnp.int32, (SG_CHUNK, SG_CHUNK), 0)
    col = lax.broadcasted_iota(jnp.int32, (SG_CHUNK, SG_CHUNK), 1)
    causal = col <= row
    wcat = jnp.concatenate(
        [jnp.where(causal, w_ref[g], 0.0).astype(BF16) for g in range(4)], axis=1)
    bias = bias_ref[...]
    for c in range(tb // SG_CHUNK):
        sl = slice(c * SG_CHUNK, (c + 1) * SG_CHUNK)
        vc = vn[sl, :]
        vstack = jnp.concatenate(
            [jnp.where(group_mask[g], vc, jnp.zeros_like(vc)) for g in range(4)], axis=0)
        mixed = _dot(wcat, vstack) + bias
        o_ref[sl, :] = (u[sl, :] * mixed).astype(o_ref.dtype)


def _spatial_gating(h, ln_g, ln_b, w_s, bias_full, tb):
    n = h.shape[0]
    cu, cv = COL_B // MIX_W, COL_B // MIX_W + 1
    return pl.pallas_call(
        functools.partial(_sg_kernel, tb=tb),
        grid=(n // tb,),
        in_specs=[pl.BlockSpec((tb, MIX_W), lambda i: (i, cu)),
                  pl.BlockSpec((tb, MIX_W), lambda i: (i, cv)),
                  pl.BlockSpec((1, MIX_W), lambda i: (0, 0)),
                  pl.BlockSpec((1, MIX_W), lambda i: (0, 0)),
                  pl.BlockSpec((4, SG_CHUNK, SG_CHUNK), lambda i: (0, 0, 0)),
                  pl.BlockSpec((SG_CHUNK, MIX_W), lambda i: (0, 0))],
        out_specs=pl.BlockSpec((tb, MIX_W), lambda i: (i, 0)),
        out_shape=jax.ShapeDtypeStruct((n, MIX_W), BF16),
        compiler_params=_cparams("parallel"),
        name="spatial_gating",
    )(h, h, ln_g, ln_b, w_s, bias_full)


BAND_TQ = 256
BAND_PREV = BAND_LEFT * BAND_CHUNK
BAND_WIN = BAND_PREV + BAND_TQ


def _band_kernel(q_ref, k2_ref, k1_ref, k0_ref, v2_ref, v1_ref, v0_ref, bias_ref, o_ref, *, nblk):
    bi = pl.program_id(0) % nblk
    lane = lax.broadcasted_iota(jnp.int32, (1, MIX_W), 1)
    q = q_ref[...]
    kcat = jnp.concatenate([k2_ref[...], k1_ref[...], k0_ref[...]], axis=0)
    vcat = jnp.concatenate([v2_ref[...], v1_ref[...], v0_ref[...]], axis=0)
    col = lax.broadcasted_iota(jnp.int32, (1, BAND_WIN), 1)
    in_seq = col >= (2 - jnp.minimum(bi, 2)) * BAND_TQ
    out = jnp.zeros((BAND_TQ, MIX_W), F32)
    for h in range(4):
        hm = (lane // HEAD_DIM) == h
        s = _nt_dot(jnp.where(hm, q, jnp.zeros_like(q)), kcat) + bias_ref[h]
        s = jnp.where(in_seq, s, NEG_BIG)
        m = jnp.max(s, axis=-1, keepdims=True)
        p = jnp.exp(s - m)
        l = jnp.sum(p, axis=-1, keepdims=True)
        o = _dot(p.astype(BF16), vcat) / l
        out = jnp.where(hm, o, out)
    o_ref[...] = out.astype(o_ref.dtype)


def _band_attention(h, bias_full, seq):
    n = h.shape[0]
    nblk = seq // BAND_TQ
    cq, ck, cv = (COL_C // MIX_W + i for i in range(3))

    def prev(i, d):
        return i - jnp.minimum(i % nblk, d)

    def spec(c, d):
        return pl.BlockSpec((BAND_TQ, MIX_W), lambda i: (prev(i, d), c))

    return pl.pallas_call(
        functools.partial(_band_kernel, nblk=nblk),
        grid=(n // BAND_TQ,),
        in_specs=[spec(cq, 0), spec(ck, 2), spec(ck, 1), spec(ck, 0),
                  spec(cv, 2), spec(cv, 1), spec(cv, 0),
                  pl.BlockSpec((4, BAND_TQ, BAND_WIN), lambda i: (0, 0, 0))],
        out_specs=pl.BlockSpec((BAND_TQ, MIX_W), lambda i: (i, 0)),
        out_shape=jax.ShapeDtypeStruct((n, MIX_W), BF16),
        compiler_params=_cparams("parallel"),
        name="band_attention",
    )(h, h, h, h, h, h, h, bias_full)


def _band_bias(rel_bias):
    period = 1024
    u = jnp.arange(period)
    rel = jnp.where(u <= BAND_WIN, BAND_PREV - u, BAND_PREV + period - u)
    row0 = rel_bias.astype(F32)[:, jnp.clip(rel, -(BAND_CHUNK - 1), BAND_REL_MAX) + (BAND_CHUNK - 1)]
    skew = jnp.tile(row0, (1, BAND_TQ))[:, :BAND_TQ * (period - 1)]
    bias = skew.reshape(4, BAND_TQ, period - 1)[:, :, :BAND_WIN]
    tc = (jnp.arange(BAND_TQ)[:, None] + BAND_PREV) // BAND_CHUNK
    sc = jnp.arange(BAND_WIN)[None, :] // BAND_CHUNK
    in_band = jnp.logical_and(sc <= tc, sc >= tc - BAND_LEFT)
    return jnp.where(in_band[None], bias, NEG_BIG)


def _ssd_kernel(z_ref, xs_ref, bc_ref, dt_ref, cwx_ref, cbx_ref, cwb_ref, cbb_ref, dtb_ref,
                alog_ref, dsk_ref, ng_ref, o_ref, xpad_ref, bpad_ref, state_ref, *, q):
    c = pl.program_id(1)

    @pl.when(c == 0)
    def _():
        xpad_ref[0:8, :] = jnp.zeros((8, SSD_INNER), F32)
        bpad_ref[0:8, :] = jnp.zeros((8, SSD_BC), F32)
        state_ref[...] = jnp.zeros_like(state_ref)

    xpad_ref[8:8 + q, :] = xs_ref[...].astype(F32)
    bpad_ref[8:8 + q, :] = bc_ref[...].astype(F32)

    def conv_silu(pad_ref, w_ref, b_ref):
        xp = pad_ref[...]
        acc = b_ref[...] + w_ref[3:4, :] * xp[8:8 + q, :]
        for d in (1, 2, 3):
            acc = acc + w_ref[3 - d:4 - d, :] * pltpu.roll(xp, d, axis=0)[8:8 + q, :]
        return _silu(acc)

    xs = conv_silu(xpad_ref, cwx_ref, cbx_ref)
    bc = conv_silu(bpad_ref, cwb_ref, cbb_ref)
    xpad_ref[0:8, :] = xpad_ref[q:q + 8, :]
    bpad_ref[0:8, :] = bpad_ref[q:q + 8, :]
    bm = bc[:, 0:128].astype(BF16)
    cm = bc[:, 128:256].astype(BF16)

    r128 = lax.broadcasted_iota(jnp.int32, (128, SSD_INNER), 0)
    c512 = lax.broadcasted_iota(jnp.int32, (128, SSD_INNER), 1)
    expand = jnp.where(c512 // HEAD_DIM == r128, 1.0, 0.0).astype(BF16)
    pick = jnp.where(c512 == r128 * HEAD_DIM, 1.0, 0.0).astype(BF16)
    dt = _softplus(_dot(dt_ref[...], expand) + dtb_ref[...])
    da = dt * (-jnp.exp(alog_ref[...]))
    row = lax.broadcasted_iota(jnp.int32, (q, q), 0)
    col = lax.broadcasted_iota(jnp.int32, (q, q), 1)
    causal = col <= row
    tri = jnp.where(causal, 1.0, 0.0).astype(BF16)
    da_hi, da_lo = _split2(da)
    acs = _dot(tri, da_hi) + _dot(tri, da_lo)
    a_hi, a_mid, a_lo = _split3(acs)
    acs_t = _nt_dot(pick, a_hi) + _nt_dot(pick, a_mid) + _nt_dot(pick, a_lo)
    xdt = xs * dt

    eye = jnp.where(lax.broadcasted_iota(jnp.int32, (128, 128), 0)
                    == lax.broadcasted_iota(jnp.int32, (128, 128), 1), 1.0, 0.0).astype(BF16)
    bm_t = _nt_dot(eye, bm).astype(BF16)
    lane128 = lax.broadcasted_iota(jnp.int32, (1, 128), 1)
    lane256 = lax.broadcasted_iota(jnp.int32, (1, 256), 1)

    y_groups = []
    for g in range(2):
        gm = (lane128 // HEAD_DIM) == g
        cb = _nt_dot(jnp.where(gm, cm, jnp.zeros_like(cm)), bm)
        xg = xdt[:, g * 256:(g + 1) * 256].astype(BF16)
        ms, xstack = [], []
        for hh in range(4):
            head = g * 4 + hh
            seg = acs[:, head * HEAD_DIM:head * HEAD_DIM + 1] - acs_t[head:head + 1, :]
            decay = jnp.exp(jnp.where(causal, seg, NEG_BIG))
            ms.append((cb * decay).astype(BF16))
            xstack.append(jnp.where((lane256 // HEAD_DIM) == hh, xg, jnp.zeros_like(xg)))
        y_groups.append(_dot(jnp.concatenate(ms, axis=1), jnp.concatenate(xstack, axis=0)))
    y_diag = jnp.concatenate(y_groups, axis=1)

    state = state_ref[...]
    y_off = _dot(cm, state.astype(BF16)) * jnp.exp(acs)
    acs_last = acs[q - 1:q, :]
    xw = (xdt * jnp.exp(acs_last - acs)).astype(BF16)
    keep = (lax.broadcasted_iota(jnp.int32, (128, SSD_INNER), 0) // HEAD_DIM
            == lax.broadcasted_iota(jnp.int32, (128, SSD_INNER), 1) // 256)
    state_ref[...] = jnp.where(keep, state * jnp.exp(acs_last) + _dot(bm_t, xw), 0.0)

    y = y_diag + y_off + xs * dsk_ref[...]
    y = y * _silu(z_ref[...].astype(F32))
    outs = []
    for g in range(2):
        yg = y[:, g * 256:(g + 1) * 256]
        outs.append(yg * lax.rsqrt(jnp.mean(yg * yg, axis=-1, keepdims=True) + LN_EPS))
    o_ref[...] = (jnp.concatenate(outs, axis=1) * ng_ref[...]).astype(o_ref.dtype)


def _ssd(h, conv_wx, conv_bx, conv_wb, conv_bb, dt_bias_e, a_log_e, d_e, norm_g, bsz, seq, q):
    n = bsz * seq
    nc = seq // q

    def tok(width, colblk):
        return pl.BlockSpec((q, width), lambda b, c: (b * nc + c, colblk))

    def const(shape):
        return pl.BlockSpec(shape, lambda b, c: (0, 0))

    return pl.pallas_call(
        functools.partial(_ssd_kernel, q=q),
        grid=(bsz, nc),
        in_specs=[tok(SSD_INNER, COL_Z // SSD_INNER), tok(SSD_INNER, COL_XS // SSD_INNER),
                  tok(SSD_BC, COL_BC // SSD_BC), tok(128, COL_DT // 128),
                  const((4, SSD_INNER)), const((1, SSD_INNER)),
                  const((4, SSD_BC)), const((1, SSD_BC)),
                  const((1, SSD_INNER)), const((1, SSD_INNER)), const((1, SSD_INNER)),
                  const((1, SSD_INNER))],
        out_specs=pl.BlockSpec((q, SSD_INNER), lambda b, c: (b * nc + c, 0)),
        out_shape=jax.ShapeDtypeStruct((n, SSD_INNER), BF16),
        scratch_shapes=[pltpu.VMEM((q + 8, SSD_INNER), F32), pltpu.VMEM((q + 8, SSD_BC), F32),
                        pltpu.VMEM((128, SSD_INNER), F32)],
        compiler_params=_cparams("parallel", "arbitrary"),
        name="ssd",
    )(h, h, h, h, conv_wx, conv_bx, conv_wb, conv_bb, dt_bias_e, a_log_e, d_e, norm_g)


def _merge_kernel(x_ref, g_ref, ya_ref, yb_ref, yc_ref, yd_ref, wa_ref, wb_ref, wc_ref, wd_ref,
                  wo_ref, lg_ref, lb_ref, o_ref):
    merged = None
    for i, (y_ref, w_ref) in enumerate(((ya_ref, wa_ref), (yb_ref, wb_ref), (yc_ref, wc_ref),
                                        (yd_ref, wd_ref))):
        gate = jax.nn.sigmoid(g_ref[:, i * D_MODEL:(i + 1) * D_MODEL].astype(F32))
        term = gate * _dot(y_ref[...], w_ref[...])
        merged = term if merged is None else merged + term
    o = DEEPNORM_ALPHA * x_ref[...] + _dot(merged.astype(BF16), wo_ref[...])
    o_ref[...] = _layer_norm(o, lg_ref[...], lb_ref[...])


def _merge(x2, h, ya, yb, yc, yd, wa, wb, wc, wd, wo, lg, lb, tm):
    n = x2.shape[0]

    def tok(width):
        return pl.BlockSpec((tm, width), lambda i: (i, 0))

    def const(shape):
        return pl.BlockSpec(shape, lambda i: (0, 0))

    return pl.pallas_call(
        _merge_kernel,
        grid=(n // tm,),
        in_specs=[tok(D_MODEL), tok(4 * D_MODEL), tok(MIX_W), tok(MIX_W), tok(MIX_W),
                  tok(SSD_INNER), const((MIX_W, D_MODEL)), const((MIX_W, D_MODEL)),
                  const((MIX_W, D_MODEL)), const((SSD_INNER, D_MODEL)),
                  const((D_MODEL, D_MODEL)), const((1, D_MODEL)), const((1, D_MODEL))],
        out_specs=tok(D_MODEL),
        out_shape=jax.ShapeDtypeStruct((n, D_MODEL), F32),
        compiler_params=_cparams("parallel"),
        name="merge_ln1",
    )(x2, h, ya, yb, yc, yd, wa, wb, wc, wd, wo, lg, lb)


META_E0, META_E1, META_RANK0, META_RANK1, META_G0, META_G1 = range(6)

SLAB = (8, 128)


def _to_slabs(x, slab_ref):
    slab_ref[...] = x.reshape((x.shape[0],) + SLAB)


def _from_slabs(slab_ref):
    return slab_ref[...].reshape(slab_ref.shape[0], SLAB[0] * SLAB[1])


def _router_kernel(x_ref, w_ref, meta_ref, cnt_ref, slab_ref, base_ref, *, tm):
    @pl.when(pl.program_id(0) == 0)
    def _():
        base_ref[...] = jnp.zeros_like(base_ref)

    x = x_ref[...]
    _to_slabs(x, slab_ref)
    logits = jnp.dot(x, w_ref[...], preferred_element_type=F32,
                     precision=lax.Precision.HIGHEST)
    lane = lax.broadcasted_iota(jnp.int32, logits.shape, 1)
    lg = jnp.where(lane < N_EXPERTS, logits, -jnp.inf)
    m1 = jnp.max(lg, axis=-1, keepdims=True)
    i1 = jnp.min(jnp.where(lg == m1, lane, 128), axis=-1, keepdims=True)
    first = lane == i1
    lg2 = jnp.where(first, -jnp.inf, lg)
    m2 = jnp.max(lg2, axis=-1, keepdims=True)
    i2 = jnp.min(jnp.where(lg2 == m2, lane, 128), axis=-1, keepdims=True)
    second = lane == i2
    e2 = jnp.exp(m2 - m1)
    denom = 1.0 + e2
    sel = jnp.where(jnp.logical_or(first, second), 1.0, 0.0)
    row = lax.broadcasted_iota(jnp.int32, (tm, tm), 0)
    col = lax.broadcasted_iota(jnp.int32, (tm, tm), 1)
    strict_lower = jnp.where(col < row, 1.0, 0.0).astype(BF16)
    pos = base_ref[...] + _dot(strict_lower, sel.astype(BF16))
    rank0 = jnp.sum(jnp.where(first, pos, 0.0), axis=-1, keepdims=True)
    rank1 = jnp.sum(jnp.where(second, pos, 0.0), axis=-1, keepdims=True)
    base_ref[...] += jnp.sum(sel, axis=0, keepdims=True)
    cnt_ref[...] = jnp.broadcast_to(base_ref[...], cnt_ref.shape)
    fields = (i1.astype(F32), i2.astype(F32), rank0, rank1, 1.0 / denom, e2 / denom)
    meta = jnp.zeros(logits.shape, F32)
    for k, val in enumerate(fields):
        meta = jnp.where(lane == k, val, meta)
    meta_ref[...] = meta


def _router(x2, w_router_padded, tm):
    n = x2.shape[0]
    return pl.pallas_call(
        functools.partial(_router_kernel, tm=tm),
        grid=(n // tm,),
        in_specs=[pl.BlockSpec((tm, D_MODEL), lambda i: (i, 0)),
                  pl.BlockSpec((D_MODEL, 128), lambda i: (0, 0))],
        out_specs=[pl.BlockSpec((tm, 128), lambda i: (i, 0)),
                   pl.BlockSpec((8, 128), lambda i: (0, 0)),
                   pl.BlockSpec((tm,) + SLAB, lambda i: (i, 0, 0))],
        out_shape=[jax.ShapeDtypeStruct((n, 128), F32), jax.ShapeDtypeStruct((8, 128), F32),
                   jax.ShapeDtypeStruct((n,) + SLAB, F32)],
        scratch_shapes=[pltpu.VMEM((1, 128), F32)],
        compiler_params=_cparams("arbitrary"),
        name="router",
    )(x2, w_router_padded)


GATHER_WINDOW = 128


def _gather_rows(table, idx):
    n = idx.shape[0]
    mesh = plsc.VectorSubcoreMesh(core_axis_name="core", subcore_axis_name="subcore")

    @pl.kernel(out_type=jax.ShapeDtypeStruct((n, table.shape[1]), table.dtype), mesh=mesh)
    def gather(table_hbm, idx_hbm, out_hbm):
        def body(idx_vmem, out_vmem):
            pltpu.sync_copy(table_hbm.at[idx_vmem.at[0]], out_vmem)

        pltpu.emit_pipeline(
            body,
            grid=(n // GATHER_WINDOW,),
            in_specs=[pl.BlockSpec((1, GATHER_WINDOW), index_map=lambda i: (0, i))],
            out_specs=[pl.BlockSpec((GATHER_WINDOW, table.shape[1]), index_map=lambda i: (i, 0))],
            core_axis_name=("core", "subcore"),
            dimension_semantics=(pltpu.PARALLEL,),
        )(idx_hbm, out_hbm)

    return gather(table, idx.reshape(1, n))


def _scatter_rows(src, idx, out_rows):
    n = idx.shape[0]
    src_blocks = src.shape[0] // GATHER_WINDOW
    mesh = plsc.VectorSubcoreMesh(core_axis_name="core", subcore_axis_name="subcore")

    @pl.kernel(out_type=jax.ShapeDtypeStruct((out_rows, src.shape[1]), src.dtype), mesh=mesh,
               scratch_types=[])
    def scatter(src_hbm, idx_hbm, out_hbm):
        def body(src_vmem, idx_vmem):
            pltpu.sync_copy(src_vmem, out_hbm.at[idx_vmem.at[0]])

        pltpu.emit_pipeline(
            body,
            grid=(n // GATHER_WINDOW,),
            in_specs=[pl.BlockSpec((GATHER_WINDOW, src.shape[1]),
                                   index_map=lambda i: (i % src_blocks, 0)),
                      pl.BlockSpec((1, GATHER_WINDOW), index_map=lambda i: (0, i))],
            out_specs=[],
            core_axis_name=("core", "subcore"),
            dimension_semantics=(pltpu.PARALLEL,),
        )(src_hbm, idx_hbm)

    return scatter(src, idx.reshape(1, n))


def _slab_row_index(rows):
    sub = jnp.arange(SLAB[0], dtype=jnp.int32)
    return (rows[:, None] * SLAB[0] + sub[None, :]).reshape(-1)


def _scatter_slabs(slabs, rows, out_rows):
    out = _scatter_rows(slabs.reshape(-1, SLAB[1]), _slab_row_index(rows), out_rows * SLAB[0])
    return out.reshape((out_rows,) + SLAB)


def _gather_slabs(slabs, rows):
    out = _gather_rows(slabs.reshape(-1, SLAB[1]), _slab_row_index(rows))
    return out.reshape((rows.shape[0],) + SLAB)


def _gmm_kernel(te_ref, xs_ref, wg_ref, wu_ref, wd_ref, o_ref, acc_ref, xb_ref, *, n_tiles):
    i = pl.program_id(0)
    f = pl.program_id(1)

    @pl.when(f == 0)
    def _():
        acc_ref[...] = jnp.zeros_like(acc_ref)
        row = lax.broadcasted_iota(jnp.int32, (xb_ref.shape[0], 1), 0)
        live = row < te_ref[n_tiles + 1 + i]
        xb_ref[...] = jnp.where(live, _from_slabs(xs_ref), 0.0).astype(BF16)

    @pl.when(i < te_ref[n_tiles])
    def _():
        xb = xb_ref[...]
        hid = _silu(_dot(xb, wg_ref[...])) * _dot(xb, wu_ref[...])
        acc_ref[...] += _dot(hid.astype(BF16), wd_ref[...])

    @pl.when(f == pl.num_programs(1) - 1)
    def _():
        _to_slabs(acc_ref[...], o_ref)


def _grouped_swiglu(tile_expert, xs, wg, wu, wd, tm, tf):
    m = xs.shape[0]
    d_ff = wg.shape[2]
    n_tiles = m // tm
    grid_spec = pltpu.PrefetchScalarGridSpec(
        num_scalar_prefetch=1,
        grid=(n_tiles, d_ff // tf),
        in_specs=[pl.BlockSpec((tm,) + SLAB, lambda i, f, te: (i, 0, 0)),
                  pl.BlockSpec((None, D_MODEL, tf), lambda i, f, te: (te[i], 0, f)),
                  pl.BlockSpec((None, D_MODEL, tf), lambda i, f, te: (te[i], 0, f)),
                  pl.BlockSpec((None, tf, D_MODEL), lambda i, f, te: (te[i], f, 0))],
        out_specs=pl.BlockSpec((tm,) + SLAB, lambda i, f, te: (i, 0, 0)),
        scratch_shapes=[pltpu.VMEM((tm, D_MODEL), F32), pltpu.VMEM((tm, D_MODEL), BF16)])
    return pl.pallas_call(
        functools.partial(_gmm_kernel, n_tiles=n_tiles),
        grid_spec=grid_spec,
        out_shape=jax.ShapeDtypeStruct((m,) + SLAB, F32),
        compiler_params=_cparams("parallel", "arbitrary"),
        name="grouped_swiglu",
    )(tile_expert, xs, wg, wu, wd)


def _combine_kernel(x_ref, y0_ref, y1_ref, meta_ref, p_ref, pg_ref, pp_ref, lg_ref, lb_ref, o_ref):
    x = x_ref[...]
    meta = meta_ref[...]
    g0 = meta[:, META_G0:META_G0 + 1]
    g1 = meta[:, META_G1:META_G1 + 1]
    ple = (jax.nn.sigmoid(_dot(x.astype(BF16), pg_ref[...]))
           * _dot(p_ref[...].astype(BF16), pp_ref[...]))
    o = DEEPNORM_ALPHA * x + g0 * _from_slabs(y0_ref) + g1 * _from_slabs(y1_ref) + ple
    o_ref[...] = _layer_norm(o, lg_ref[...], lb_ref[...])


def _combine_ln2(x2, gathered, meta, p2, pg, pp, lg, lb, tm):
    n = x2.shape[0]
    ple_dim = p2.shape[1]
    nb = n // tm

    def tok(width):
        return pl.BlockSpec((tm, width), lambda i: (i, 0))

    def const(shape):
        return pl.BlockSpec(shape, lambda i: (0, 0))

    return pl.pallas_call(
        _combine_kernel,
        grid=(nb,),
        in_specs=[tok(D_MODEL), pl.BlockSpec((tm,) + SLAB, lambda i: (i, 0, 0)),
                  pl.BlockSpec((tm,) + SLAB, lambda i: (i + nb, 0, 0)),
                  tok(128), tok(ple_dim), const((D_MODEL, D_MODEL)), const((ple_dim, D_MODEL)),
                  const((1, D_MODEL)), const((1, D_MODEL))],
        out_specs=tok(D_MODEL),
        out_shape=jax.ShapeDtypeStruct((n, D_MODEL), F32),
        compiler_params=_cparams("parallel"),
        name="combine_ple_ln2",
    )(x2, gathered, gathered, meta, p2, pg, pp, lg, lb)


def _moe(x2, w_router, wg, wu, wd, p2, pg, pp, lg, lb, tm, tg, tf):
    n = x2.shape[0]
    wr = jnp.pad(w_router, ((0, 0), (0, 128 - N_EXPERTS)))
    meta, counts, x_slabs = _router(x2, wr, tm)
    cnt = counts[0, :N_EXPERTS].astype(jnp.int32)
    padded = ((cnt + tg - 1) // tg) * tg
    ends = jnp.cumsum(padded)
    starts = ends - padded
    experts = jnp.arange(N_EXPERTS, dtype=jnp.int32)

    def dest(e_lane, rank_lane):
        e = meta[:, e_lane].astype(jnp.int32)
        start = jnp.sum(jnp.where(e[:, None] == experts[None, :], starts[None, :], 0), axis=1)
        return start + meta[:, rank_lane].astype(jnp.int32)

    dest01 = jnp.concatenate([dest(META_E0, META_RANK0), dest(META_E1, META_RANK1)])
    m = 2 * n + N_EXPERTS * tg
    n_tiles = m // tg
    tile_start = jnp.arange(n_tiles, dtype=jnp.int32) * tg
    tile_expert = jnp.minimum(
        jnp.sum((ends[None, :] <= tile_start[:, None]).astype(jnp.int32), axis=1), N_EXPERTS - 1)
    onehot = tile_expert[:, None] == experts[None, :]
    tile_end = jnp.sum(jnp.where(onehot, (starts + cnt)[None, :], 0), axis=1)
    live_rows = jnp.clip(tile_end - tile_start, 0, tg)
    prefetch = jnp.concatenate([tile_expert, (ends[-1] // tg)[None], live_rows]).astype(jnp.int32)
    xs = _scatter_slabs(x_slabs, dest01, m)
    ys = _grouped_swiglu(prefetch, xs, wg, wu, wd, tg, tf)
    gathered = _gather_slabs(ys, dest01)
    return _combine_ln2(x2, gathered, meta, p2, pg, pp, lg, lb, tm)


def _ffn_kernel(x_ref, wg_ref, wu_ref, wd_ref, p_ref, pg_ref, pp_ref, lg_ref, lb_ref,
                o_ref, acc_ref, xb_ref):
    f = pl.program_id(1)

    @pl.when(f == 0)
    def _():
        acc_ref[...] = jnp.zeros_like(acc_ref)
        xb_ref[...] = x_ref[...].astype(BF16)

    xb = xb_ref[...]
    hid = _silu(_dot(xb, wg_ref[...])) * _dot(xb, wu_ref[...])
    acc_ref[...] += _dot(hid.astype(BF16), wd_ref[...])

    @pl.when(f == pl.num_programs(1) - 1)
    def _():
        ple = (jax.nn.sigmoid(_dot(xb, pg_ref[...]))
               * _dot(p_ref[...].astype(BF16), pp_ref[...]))
        o = DEEPNORM_ALPHA * x_ref[...] + acc_ref[...] + ple
        o_ref[...] = _layer_norm(o, lg_ref[...], lb_ref[...])


def _ffn(x2, wg, wu, wd, p2, pg, pp, lg, lb, tm, tf):
    n = x2.shape[0]
    d_ff = wg.shape[1]
    ple_dim = p2.shape[1]

    def tok(width):
        return pl.BlockSpec((tm, width), lambda i, f: (i, 0))

    def const(shape):
        return pl.BlockSpec(shape, lambda i, f: (0, 0))

    return pl.pallas_call(
        _ffn_kernel,
        grid=(n // tm, d_ff // tf),
        in_specs=[tok(D_MODEL),
                  pl.BlockSpec((D_MODEL, tf), lambda i, f: (0, f)),
                  pl.BlockSpec((D_MODEL, tf), lambda i, f: (0, f)),
                  pl.BlockSpec((tf, D_MODEL), lambda i, f: (f, 0)),
                  tok(ple_dim), const((D_MODEL, D_MODEL)), const((ple_dim, D_MODEL)),
                  const((1, D_MODEL)), const((1, D_MODEL))],
        out_specs=tok(D_MODEL),
        out_shape=jax.ShapeDtypeStruct((n, D_MODEL), F32),
        scratch_shapes=[pltpu.VMEM((tm, D_MODEL), F32), pltpu.VMEM((tm, D_MODEL), BF16)],
        compiler_params=_cparams("parallel", "arbitrary"),
        name="ffn_ple_ln2",
    )(x2, wg, wu, wd, p2, pg, pp, lg, lb)


def _prep_w_in(w):
    a, b, c, d, g = jnp.split(w, [768, 1280, 2048, 3336], axis=1)
    z, xs, bc, dt = jnp.split(d, [512, 1024, 1280], axis=1)

    def scale_q(t, scale):
        return jnp.concatenate([t[:, :MIX_W] * scale, t[:, MIX_W:]], axis=1)

    qk_scale = HEAD_DIM ** -0.5
    pad = jnp.zeros((w.shape[0], IN_COLS - (COL_DT + 8)), w.dtype)
    return jnp.concatenate([g, scale_q(a, qk_scale * LOG2_E), b, scale_q(c, qk_scale), z, xs, bc, dt,
                            pad], axis=1).astype(BF16)


def _row(v):
    return v.reshape(1, -1).astype(F32)


def _per_head(v):
    return _row(jnp.repeat(v, HEAD_DIM))


def kernel(x, p, w_in, w_br_a, w_br_b, w_br_c, w_br_d, w_out, sg_ln_g, sg_ln_b, sg_w, sg_b,
           ca_rel_bias, ssd_conv_w, ssd_conv_b, ssd_dt_bias, ssd_a_log, ssd_d, ssd_norm_g,
           ln1_g, ln1_b, ffn_w_gate, ffn_w_up, ffn_w_down, moe_router, moe_w_gate, moe_w_up,
           moe_w_down, ple_w_gate, ple_w_proj, ln2_g, ln2_b):
    bsz, seq, _ = x.shape
    n = bsz * seq
    x2 = x.reshape(n, D_MODEL)
    tm_proj = min(1024, n)
    tm = min(512, n)
    sb_blk = min(256, seq)
    ssd_q = min(256, seq)
    sg_tb = min(1024, seq)

    for i in range(DEPTH):
        h = _inproj(x2, _prep_w_in(w_in[i]), tm_proj, 3840)
        ya = _stick_breaking(h, bsz, seq, sb_blk)
        sg_bias = jnp.repeat(jnp.transpose(sg_b[i]), HEAD_DIM, axis=1).astype(F32)
        yb = _spatial_gating(h, _row(sg_ln_g[i]), _row(sg_ln_b[i]), sg_w[i], sg_bias, sg_tb)
        yc = _band_attention(h, _band_bias(ca_rel_bias[i]), seq)
        cw, cb = ssd_conv_w[i], ssd_conv_b[i]
        yd = _ssd(h, cw[:, :SSD_INNER], _row(cb[:SSD_INNER]), cw[:, SSD_INNER:],
                  _row(cb[SSD_INNER:]), _per_head(ssd_dt_bias[i]), _per_head(ssd_a_log[i]),
                  _per_head(ssd_d[i]), _row(ssd_norm_g[i]), bsz, seq, ssd_q)
        x2 = _merge(x2, h, ya, yb, yc, yd, w_br_a[i].astype(BF16), w_br_b[i].astype(BF16),
                    w_br_c[i].astype(BF16), w_br_d[i].astype(BF16), w_out[i].astype(BF16),
                    _row(ln1_g[i]), _row(ln1_b[i]), tm_proj)
        p2 = p[i].reshape(n, -1)
        pg, pp = ple_w_gate[i].astype(BF16), ple_w_proj[i].astype(BF16)
        j = i // 2
        if i % 2 == 0:
            x2 = _ffn(x2, ffn_w_gate[j].astype(BF16), ffn_w_up[j].astype(BF16),
                      ffn_w_down[j].astype(BF16), p2, pg, pp, _row(ln2_g[i]), _row(ln2_b[i]),
                      tm, 2816)
        else:
            x2 = _moe(x2, moe_router[j], moe_w_gate[j].astype(BF16), moe_w_up[j].astype(BF16),
                      moe_w_down[j].astype(BF16), p2, pg, pp, _row(ln2_g[i]), _row(ln2_b[i]),
                      tm, min(1024, n), 1792)
    return x2.reshape(bsz, seq, D_MODEL)
```

```python
import functools

import jax
import jax.numpy as jnp
from jax import lax
from jax.experimental import pallas as pl
from jax.experimental.pallas import tpu as pltpu
from jax.experimental.pallas import tpu_sc as plsc

F32 = jnp.float32
BF16 = jnp.bfloat16

D_MODEL = 1024
DEPTH = 2
LN_EPS = 1e-5
DEEPNORM_ALPHA = (2 * DEPTH) ** 0.25
HEAD_DIM = 64
MIX_W = 256
SG_CHUNK = 128
BAND_CHUNK = 64
BAND_LEFT = 8
BAND_REL_MAX = 256
SSD_INNER = 512
SSD_HEADS = 8
SSD_BC = 256
N_EXPERTS = 8
NEG_BIG = -1e30
LOG2_E = 1.4426950408889634
EXP2_UNDERFLOW = -160.0

COL_A = 0
COL_B = 768
COL_C = 1280
COL_Z = 2048
COL_XS = 2560
COL_BC = 3072
COL_DT = 3328
IN_COLS = 3584

VMEM_LIMIT = 56 * 1024 * 1024


def _cparams(*sem):
    return pltpu.CompilerParams(dimension_semantics=sem, vmem_limit_bytes=VMEM_LIMIT)


def _nt_dot(a, b):
    return lax.dot_general(a, b, (((1,), (1,)), ((), ())), preferred_element_type=F32)


def _dot(a, b):
    return jnp.dot(a, b, preferred_element_type=F32)


def _softplus(x):
    return jnp.maximum(x, 0.0) + jnp.log(1.0 + jnp.exp(-jnp.abs(x)))


def _softplus2(x):
    return jnp.maximum(x, 0.0) + jnp.log2(1.0 + jnp.exp2(-jnp.abs(x)))


def _silu(x):
    return x * jax.nn.sigmoid(x)


def _split2(x):
    hi = x.astype(BF16)
    lo = (x - hi.astype(F32)).astype(BF16)
    return hi, lo


def _split3(x):
    hi = x.astype(BF16)
    r = x - hi.astype(F32)
    mid = r.astype(BF16)
    lo = (r - mid.astype(F32)).astype(BF16)
    return hi, mid, lo


def _layer_norm(x, g, b):
    mu = jnp.mean(x, axis=-1, keepdims=True)
    xc = x - mu
    var = jnp.mean(xc * xc, axis=-1, keepdims=True)
    return xc * lax.rsqrt(var + LN_EPS) * g + b


def _inproj_kernel(x_ref, w_ref, o_ref):
    o_ref[...] = _dot(x_ref[...].astype(BF16), w_ref[...]).astype(o_ref.dtype)


def _inproj(x2, w, tm, tn):
    n, d = x2.shape
    nc = w.shape[1]
    return pl.pallas_call(
        _inproj_kernel,
        grid=(n // tm, nc // tn),
        in_specs=[pl.BlockSpec((tm, d), lambda i, j: (i, 0)),
                  pl.BlockSpec((d, tn), lambda i, j: (0, j))],
        out_specs=pl.BlockSpec((tm, tn), lambda i, j: (i, j)),
        out_shape=jax.ShapeDtypeStruct((n, nc), BF16),
        compiler_params=_cparams("parallel", "arbitrary"),
        name="inproj",
    )(x2, w)


def _sb_kernel(q_ref, k_ref, v_ref, o_ref, acc_ref, run_ref, kmax_ref, *, blk, seq):
    qi = pl.program_id(1)
    lane = lax.broadcasted_iota(jnp.int32, (1, MIX_W), 1)
    head_mask = [(lane // HEAD_DIM) == h for h in range(4)]

    def head_sq_norms(t):
        tf = t.astype(F32)
        sq = tf * tf
        return [jnp.sum(jnp.where(head_mask[h], sq, 0.0), axis=-1, keepdims=True)
                for h in range(4)]

    @pl.when(qi == 0)
    def _():
        kmax_ref[...] = jnp.zeros_like(kmax_ref)

        def scan(c, carry):
            norms = head_sq_norms(k_ref[pl.ds(pl.multiple_of(c * blk, blk), blk), :])
            for h in range(4):
                kmax_ref[h] = jnp.maximum(kmax_ref[h], jnp.max(norms[h], axis=0, keepdims=True))
            return carry

        lax.fori_loop(0, seq // blk, scan, 0)

    q = q_ref[...]
    q_heads = [jnp.where(head_mask[h], q, jnp.zeros_like(q)) for h in range(4)]
    q_norms = head_sq_norms(q)
    z_bound = [jnp.sqrt(q_norms[h] * kmax_ref[h]) * 1.001 + 1e-3 for h in range(4)]
    row = lax.broadcasted_iota(jnp.int32, (blk, blk), 0)
    col = lax.broadcasted_iota(jnp.int32, (blk, blk), 1)
    below_diag = col < row
    suffix = jnp.where(row >= col, 1.0, 0.0).astype(BF16)
    suffix2 = jnp.concatenate([suffix, suffix], axis=0)

    acc_ref[...] = jnp.zeros_like(acc_ref)
    run_ref[...] = jnp.zeros_like(run_ref)

    def process(j, masked):
        kb = qi - j
        start = pl.multiple_of(kb * blk, blk)
        k_blk = k_ref[pl.ds(start, blk), :]
        v_blk = v_ref[pl.ds(start, blk), :]
        weights = []
        slack = None
        for h in range(4):
            z = _nt_dot(q_heads[h], k_blk)
            sp = _softplus2(z)
            if masked:
                sp = jnp.where(below_diag, sp, 0.0)
            cs = _dot(jnp.concatenate(_split2(sp), axis=1), suffix2)
            run = run_ref[h]
            arg = z - cs - run
            if masked:
                arg = jnp.where(below_diag, arg, NEG_BIG)
            w = jnp.exp2(arg)
            run = run + cs[:, 0:1]
            run_ref[h] = run
            weights.append(w.astype(BF16))
            head_slack = jnp.max(z_bound[h] - run)
            slack = head_slack if slack is None else jnp.maximum(slack, head_slack)
        wcat = jnp.concatenate(weights, axis=1)
        vcat = jnp.concatenate(
            [jnp.where(head_mask[h], v_blk, jnp.zeros_like(v_blk)) for h in range(4)], axis=0)
        acc_ref[...] += _dot(wcat, vcat)
        return slack > EXP2_UNDERFLOW

    def cond(carry):
        j, live = carry
        return jnp.logical_and(j <= qi, live)

    def body(carry):
        j, _ = carry
        return j + 1, process(j, masked=False)

    lax.while_loop(cond, body, (jnp.int32(1), process(0, masked=True)))
    o_ref[...] = acc_ref[...].astype(o_ref.dtype)


def _stick_breaking(h, bsz, seq, blk):
    n = bsz * seq
    nq = seq // blk
    cq, ck, cv = (COL_A // MIX_W + i for i in range(3))
    return pl.pallas_call(
        functools.partial(_sb_kernel, blk=blk, seq=seq),
        grid=(bsz, nq),
        in_specs=[pl.BlockSpec((blk, MIX_W), lambda b, i: (b * nq + i, cq)),
                  pl.BlockSpec((seq, MIX_W), lambda b, i: (b, ck)),
                  pl.BlockSpec((seq, MIX_W), lambda b, i: (b, cv))],
        out_specs=pl.BlockSpec((blk, MIX_W), lambda b, i: (b * nq + i, 0)),
        out_shape=jax.ShapeDtypeStruct((n, MIX_W), BF16),
        scratch_shapes=[pltpu.VMEM((blk, MIX_W), F32), pltpu.VMEM((4, blk, 1), F32),
                        pltpu.VMEM((4, 1, 1), F32)],
        compiler_params=_cparams("parallel", "arbitrary"),
        name="stick_breaking",
    )(h, h, h)


def _gelu_tanh(x):
    return 0.5 * x * (1.0 + jnp.tanh(0.7978845608028654 * (x + 0.044715 * (x * x * x))))


def _sg_kernel(u_ref, v_ref, g_ref, b_ref, w_ref, bias_ref, o_ref, *, tb):
    u = _gelu_tanh(u_ref[...].astype(F32))
    v = _gelu_tanh(v_ref[...].astype(F32))
    vn = _layer_norm(v, g_ref[...], b_ref[...]).astype(BF16)
    lane = lax.broadcasted_iota(jnp.int32, (1, MIX_W), 1)
    group_mask = [(lane // HEAD_DIM) == g for g in range(4)]
    row = lax.broadcasted_iota(jnp.int32, (SG_CHUNK, SG_CHUNK), 0)
    col = lax.broadcasted_iota(jnp.int32, (SG_CHUNK, SG_CHUNK), 1)
    causal = col <= row
    wcat = jnp.concatenate(
        [jnp.where(causal, w_ref[g], 0.0).astype(BF16) for g in range(4)], axis=1)
    bias = bias_ref[...]
    for c in range(tb // SG_CHUNK):
        sl = slice(c * SG_CHUNK, (c + 1) * SG_CHUNK)
        vc = vn[sl, :]
        vstack = jnp.concatenate(
            [jnp.where(group_mask[g], vc, jnp.zeros_like(vc)) for g in range(4)], axis=0)
        mixed = _dot(wcat, vstack) + bias
        o_ref[sl, :] = (u[sl, :] * mixed).astype(o_ref.dtype)


def _spatial_gating(h, ln_g, ln_b, w_s, bias_full, tb):
    n = h.shape[0]
    cu, cv = COL_B // MIX_W, COL_B // MIX_W + 1
    return pl.pallas_call(
        functools.partial(_sg_kernel, tb=tb),
        grid=(n // tb,),
        in_specs=[pl.BlockSpec((tb, MIX_W), lambda i: (i, cu)),
                  pl.BlockSpec((tb, MIX_W), lambda i: (i, cv)),
                  pl.BlockSpec((1, MIX_W), lambda i: (0, 0)),
                  pl.BlockSpec((1, MIX_W), lambda i: (0, 0)),
                  pl.BlockSpec((4, SG_CHUNK, SG_CHUNK), lambda i: (0, 0, 0)),
                  pl.BlockSpec((SG_CHUNK, MIX_W), lambda i: (0, 0))],
        out_specs=pl.BlockSpec((tb, MIX_W), lambda i: (i, 0)),
        out_shape=jax.ShapeDtypeStruct((n, MIX_W), BF16),
        compiler_params=_cparams("parallel"),
        name="spatial_gating",
    )(h, h, ln_g, ln_b, w_s, bias_full)


BAND_TQ = 256
BAND_PREV = BAND_LEFT * BAND_CHUNK
BAND_WIN = BAND_PREV + BAND_TQ


def _band_kernel(q_ref, k2_ref, k1_ref, k0_ref, v2_ref, v1_ref, v0_ref, bias_ref, o_ref, *, nblk):
    bi = pl.program_id(0) % nblk
    lane = lax.broadcasted_iota(jnp.int32, (1, MIX_W), 1)
    q = q_ref[...]
    kcat = jnp.concatenate([k2_ref[...], k1_ref[...], k0_ref[...]], axis=0)
    vcat = jnp.concatenate([v2_ref[...], v1_ref[...], v0_ref[...]], axis=0)
    col = lax.broadcasted_iota(jnp.int32, (1, BAND_WIN), 1)
    in_seq = col >= (2 - jnp.minimum(bi, 2)) * BAND_TQ
    out = jnp.zeros((BAND_TQ, MIX_W), F32)
    for h in range(4):
        hm = (lane // HEAD_DIM) == h
        s = _nt_dot(jnp.where(hm, q, jnp.zeros_like(q)), kcat) + bias_ref[h]
        s = jnp.where(in_seq, s, NEG_BIG)
        m = jnp.max(s, axis=-1, keepdims=True)
        p = jnp.exp(s - m)
        l = jnp.sum(p, axis=-1, keepdims=True)
        o = _dot(p.astype(BF16), vcat) / l
        out = jnp.where(hm, o, out)
    o_ref[...] = out.astype(o_ref.dtype)


def _band_attention(h, bias_full, seq):
    n = h.shape[0]
    nblk = seq // BAND_TQ
    cq, ck, cv = (COL_C // MIX_W + i for i in range(3))

    def prev(i, d):
        return i - jnp.minimum(i % nblk, d)

    def spec(c, d):
        return pl.BlockSpec((BAND_TQ, MIX_W), lambda i: (prev(i, d), c))

    return pl.pallas_call(
        functools.partial(_band_kernel, nblk=nblk),
        grid=(n // BAND_TQ,),
        in_specs=[spec(cq, 0), spec(ck, 2), spec(ck, 1), spec(ck, 0),
                  spec(cv, 2), spec(cv, 1), spec(cv, 0),
                  pl.BlockSpec((4, BAND_TQ, BAND_WIN), lambda i: (0, 0, 0))],
        out_specs=pl.BlockSpec((BAND_TQ, MIX_W), lambda i: (i, 0)),
        out_shape=jax.ShapeDtypeStruct((n, MIX_W), BF16),
        compiler_params=_cparams("parallel"),
        name="band_attention",
    )(h, h, h, h, h, h, h, bias_full)


def _band_bias(rel_bias):
    period = 1024
    u = jnp.arange(period)
    rel = jnp.where(u <= BAND_WIN, BAND_PREV - u, BAND_PREV + period - u)
    row0 = rel_bias.astype(F32)[:, jnp.clip(rel, -(BAND_CHUNK - 1), BAND_REL_MAX) + (BAND_CHUNK - 1)]
    skew = jnp.tile(row0, (1, BAND_TQ))[:, :BAND_TQ * (period - 1)]
    bias = skew.reshape(4, BAND_TQ, period - 1)[:, :, :BAND_WIN]
    tc = (jnp.arange(BAND_TQ)[:, None] + BAND_PREV) // BAND_CHUNK
    sc = jnp.arange(BAND_WIN)[None, :] // BAND_CHUNK
    in_band = jnp.logical_and(sc <= tc, sc >= tc - BAND_LEFT)
    return jnp.where(in_band[None], bias, NEG_BIG)


def _ssd_kernel(z_ref, xs_ref, bc_ref, dt_ref, cwx_ref, cbx_ref, cwb_ref, cbb_ref, dtb_ref,
                alog_ref, dsk_ref, ng_ref, o_ref, xpad_ref, bpad_ref, state_ref, *, q):
    c = pl.program_id(1)

    @pl.when(c == 0)
    def _():
        xpad_ref[0:8, :] = jnp.zeros((8, SSD_INNER), F32)
        bpad_ref[0:8, :] = jnp.zeros((8, SSD_BC), F32)
        state_ref[...] = jnp.zeros_like(state_ref)

    xpad_ref[8:8 + q, :] = xs_ref[...].astype(F32)
    bpad_ref[8:8 + q, :] = bc_ref[...].astype(F32)

    def conv_silu(pad_ref, w_ref, b_ref):
        xp = pad_ref[...]
        acc = b_ref[...] + w_ref[3:4, :] * xp[8:8 + q, :]
        for d in (1, 2, 3):
            acc = acc + w_ref[3 - d:4 - d, :] * pltpu.roll(xp, d, axis=0)[8:8 + q, :]
        return _silu(acc)

    xs = conv_silu(xpad_ref, cwx_ref, cbx_ref)
    bc = conv_silu(bpad_ref, cwb_ref, cbb_ref)
    xpad_ref[0:8, :] = xpad_ref[q:q + 8, :]
    bpad_ref[0:8, :] = bpad_ref[q:q + 8, :]
    bm = bc[:, 0:128].astype(BF16)
    cm = bc[:, 128:256].astype(BF16)

    r128 = lax.broadcasted_iota(jnp.int32, (128, SSD_INNER), 0)
    c512 = lax.broadcasted_iota(jnp.int32, (128, SSD_INNER), 1)
    expand = jnp.where(c512 // HEAD_DIM == r128, 1.0, 0.0).astype(BF16)
    pick = jnp.where(c512 == r128 * HEAD_DIM, 1.0, 0.0).astype(BF16)
    dt = _softplus(_dot(dt_ref[...], expand) + dtb_ref[...])
    da = dt * (-jnp.exp(alog_ref[...]))
    row = lax.broadcasted_iota(jnp.int32, (q, q), 0)
    col = lax.broadcasted_iota(jnp.int32, (q, q), 1)
    causal = col <= row
    tri = jnp.where(causal, 1.0, 0.0).astype(BF16)
    da_hi, da_lo = _split2(da)
    acs = _dot(tri, da_hi) + _dot(tri, da_lo)
    a_hi, a_mid, a_lo = _split3(acs)
    acs_t = _nt_dot(pick, a_hi) + _nt_dot(pick, a_mid) + _nt_dot(pick, a_lo)
    xdt = xs * dt

    eye = jnp.where(lax.broadcasted_iota(jnp.int32, (128, 128), 0)
                    == lax.broadcasted_iota(jnp.int32, (128, 128), 1), 1.0, 0.0).astype(BF16)
    bm_t = _nt_dot(eye, bm).astype(BF16)
    lane128 = lax.broadcasted_iota(jnp.int32, (1, 128), 1)
    lane256 = lax.broadcasted_iota(jnp.int32, (1, 256), 1)

    y_groups = []
    for g in range(2):
        gm = (lane128 // HEAD_DIM) == g
        cb = _nt_dot(jnp.where(gm, cm, jnp.zeros_like(cm)), bm)
        xg = xdt[:, g * 256:(g + 1) * 256].astype(BF16)
        ms, xstack = [], []
        for hh in range(4):
            head = g * 4 + hh
            seg = acs[:, head * HEAD_DIM:head * HEAD_DIM + 1] - acs_t[head:head + 1, :]
            decay = jnp.exp(jnp.where(causal, seg, NEG_BIG))
            ms.append((cb * decay).astype(BF16))
            xstack.append(jnp.where((lane256 // HEAD_DIM) == hh, xg, jnp.zeros_like(xg)))
        y_groups.append(_dot(jnp.concatenate(ms, axis=1), jnp.concatenate(xstack, axis=0)))
    y_diag = jnp.concatenate(y_groups, axis=1)

    state = state_ref[...]
    y_off = _dot(cm, state.astype(BF16)) * jnp.exp(acs)
    acs_last = acs[q - 1:q, :]
    xw = (xdt * jnp.exp(acs_last - acs)).astype(BF16)
    keep = (lax.broadcasted_iota(jnp.int32, (128, SSD_INNER), 0) // HEAD_DIM
            == lax.broadcasted_iota(jnp.int32, (128, SSD_INNER), 1) // 256)
    state_ref[...] = jnp.where(keep, state * jnp.exp(acs_last) + _dot(bm_t, xw), 0.0)

    y = y_diag + y_off + xs * dsk_ref[...]
    y = y * _silu(z_ref[...].astype(F32))
    outs = []
    for g in range(2):
        yg = y[:, g * 256:(g + 1) * 256]
        outs.append(yg * lax.rsqrt(jnp.mean(yg * yg, axis=-1, keepdims=True) + LN_EPS))
    o_ref[...] = (jnp.concatenate(outs, axis=1) * ng_ref[...]).astype(o_ref.dtype)


def _ssd(h, conv_wx, conv_bx, conv_wb, conv_bb, dt_bias_e, a_log_e, d_e, norm_g, bsz, seq, q):
    n = bsz * seq
    nc = seq // q

    def tok(width, colblk):
        return pl.BlockSpec((q, width), lambda b, c: (b * nc + c, colblk))

    def const(shape):
        return pl.BlockSpec(shape, lambda b, c: (0, 0))

    return pl.pallas_call(
        functools.partial(_ssd_kernel, q=q),
        grid=(bsz, nc),
        in_specs=[tok(SSD_INNER, COL_Z // SSD_INNER), tok(SSD_INNER, COL_XS // SSD_INNER),
                  tok(SSD_BC, COL_BC // SSD_BC), tok(128, COL_DT // 128),
                  const((4, SSD_INNER)), const((1, SSD_INNER)),
                  const((4, SSD_BC)), const((1, SSD_BC)),
                  const((1, SSD_INNER)), const((1, SSD_INNER)), const((1, SSD_INNER)),
                  const((1, SSD_INNER))],
        out_specs=pl.BlockSpec((q, SSD_INNER), lambda b, c: (b * nc + c, 0)),
        out_shape=jax.ShapeDtypeStruct((n, SSD_INNER), BF16),
        scratch_shapes=[pltpu.VMEM((q + 8, SSD_INNER), F32), pltpu.VMEM((q + 8, SSD_BC), F32),
                        pltpu.VMEM((128, SSD_INNER), F32)],
        compiler_params=_cparams("parallel", "arbitrary"),
        name="ssd",
    )(h, h, h, h, conv_wx, conv_bx, conv_wb, conv_bb, dt_bias_e, a_log_e, d_e, norm_g)


def _merge_kernel(x_ref, wg_ref, ya_ref, yb_ref, yc_ref, yd_ref, wa_ref, wb_ref, wc_ref, wd_ref,
                  wo_ref, lg_ref, lb_ref, o_ref):
    x = x_ref[...]
    xb = x.astype(BF16)
    merged = None
    for i, (y_ref, w_ref) in enumerate(((ya_ref, wa_ref), (yb_ref, wb_ref), (yc_ref, wc_ref),
                                        (yd_ref, wd_ref))):
        gate = jax.nn.sigmoid(_dot(xb, wg_ref[:, i * D_MODEL:(i + 1) * D_MODEL]))
        term = gate * _dot(y_ref[...], w_ref[...])
        merged = term if merged is None else merged + term
    o = DEEPNORM_ALPHA * x + _dot(merged.astype(BF16), wo_ref[...])
    o_ref[...] = _layer_norm(o, lg_ref[...], lb_ref[...])


def _merge(x2, w_gate, ya, yb, yc, yd, wa, wb, wc, wd, wo, lg, lb, tm):
    n = x2.shape[0]

    def tok(width):
        return pl.BlockSpec((tm, width), lambda i: (i, 0))

    def const(shape):
        return pl.BlockSpec(shape, lambda i: (0, 0))

    return pl.pallas_call(
        _merge_kernel,
        grid=(n // tm,),
        in_specs=[tok(D_MODEL), const((D_MODEL, 4 * D_MODEL)), tok(MIX_W), tok(MIX_W), tok(MIX_W),
                  tok(SSD_INNER), const((MIX_W, D_MODEL)), const((MIX_W, D_MODEL)),
                  const((MIX_W, D_MODEL)), const((SSD_INNER, D_MODEL)),
                  const((D_MODEL, D_MODEL)), const((1, D_MODEL)), const((1, D_MODEL))],
        out_specs=tok(D_MODEL),
        out_shape=jax.ShapeDtypeStruct((n, D_MODEL), F32),
        compiler_params=_cparams("parallel"),
        name="merge_ln1",
    )(x2, w_gate, ya, yb, yc, yd, wa, wb, wc, wd, wo, lg, lb)


META_E0, META_E1, META_RANK0, META_RANK1, META_G0, META_G1 = range(6)

SLAB = (8, 128)


def _to_slabs(x, slab_ref):
    slab_ref[...] = x.reshape((x.shape[0],) + SLAB)


def _from_slabs(slab_ref):
    return slab_ref[...].reshape(slab_ref.shape[0], SLAB[0] * SLAB[1])


def _router_kernel(x_ref, w_ref, meta_ref, cnt_ref, slab_ref, base_ref, *, tm):
    @pl.when(pl.program_id(0) == 0)
    def _():
        base_ref[...] = jnp.zeros_like(base_ref)

    x = x_ref[...]
    _to_slabs(x, slab_ref)
    logits = jnp.dot(x, w_ref[...], preferred_element_type=F32,
                     precision=lax.Precision.HIGHEST)
    lane = lax.broadcasted_iota(jnp.int32, logits.shape, 1)
    lg = jnp.where(lane < N_EXPERTS, logits, -jnp.inf)
    m1 = jnp.max(lg, axis=-1, keepdims=True)
    i1 = jnp.min(jnp.where(lg == m1, lane, 128), axis=-1, keepdims=True)
    first = lane == i1
    lg2 = jnp.where(first, -jnp.inf, lg)
    m2 = jnp.max(lg2, axis=-1, keepdims=True)
    i2 = jnp.min(jnp.where(lg2 == m2, lane, 128), axis=-1, keepdims=True)
    second = lane == i2
    e2 = jnp.exp(m2 - m1)
    denom = 1.0 + e2
    sel = jnp.where(jnp.logical_or(first, second), 1.0, 0.0)
    row = lax.broadcasted_iota(jnp.int32, (tm, tm), 0)
    col = lax.broadcasted_iota(jnp.int32, (tm, tm), 1)
    strict_lower = jnp.where(col < row, 1.0, 0.0).astype(BF16)
    pos = base_ref[...] + _dot(strict_lower, sel.astype(BF16))
    rank0 = jnp.sum(jnp.where(first, pos, 0.0), axis=-1, keepdims=True)
    rank1 = jnp.sum(jnp.where(second, pos, 0.0), axis=-1, keepdims=True)
    base_ref[...] += jnp.sum(sel, axis=0, keepdims=True)
    cnt_ref[...] = jnp.broadcast_to(base_ref[...], cnt_ref.shape)
    fields = (i1.astype(F32), i2.astype(F32), rank0, rank1, 1.0 / denom, e2 / denom)
    meta = jnp.zeros(logits.shape, F32)
    for k, val in enumerate(fields):
        meta = jnp.where(lane == k, val, meta)
    meta_ref[...] = meta


def _router(x2, w_router_padded, tm):
    n = x2.shape[0]
    return pl.pallas_call(
        functools.partial(_router_kernel, tm=tm),
        grid=(n // tm,),
        in_specs=[pl.BlockSpec((tm, D_MODEL), lambda i: (i, 0)),
                  pl.BlockSpec((D_MODEL, 128), lambda i: (0, 0))],
        out_specs=[pl.BlockSpec((tm, 128), lambda i: (i, 0)),
                   pl.BlockSpec((8, 128), lambda i: (0, 0)),
                   pl.BlockSpec((tm,) + SLAB, lambda i: (i, 0, 0))],
        out_shape=[jax.ShapeDtypeStruct((n, 128), F32), jax.ShapeDtypeStruct((8, 128), F32),
                   jax.ShapeDtypeStruct((n,) + SLAB, F32)],
        scratch_shapes=[pltpu.VMEM((1, 128), F32)],
        compiler_params=_cparams("arbitrary"),
        name="router",
    )(x2, w_router_padded)


GATHER_WINDOW = 128


def _gather_rows(table, idx):
    n = idx.shape[0]
    mesh = plsc.VectorSubcoreMesh(core_axis_name="core", subcore_axis_name="subcore")

    @pl.kernel(out_type=jax.ShapeDtypeStruct((n, table.shape[1]), table.dtype), mesh=mesh)
    def gather(table_hbm, idx_hbm, out_hbm):
        def body(idx_vmem, out_vmem):
            pltpu.sync_copy(table_hbm.at[idx_vmem.at[0]], out_vmem)

        pltpu.emit_pipeline(
            body,
            grid=(n // GATHER_WINDOW,),
            in_specs=[pl.BlockSpec((1, GATHER_WINDOW), index_map=lambda i: (i, 0))],
            out_specs=[pl.BlockSpec((GATHER_WINDOW, table.shape[1]), index_map=lambda i: (i, 0))],
            core_axis_name=("core", "subcore"),
            dimension_semantics=(pltpu.PARALLEL,),
        )(idx_hbm, out_hbm)

    return gather(table, idx.reshape(n // GATHER_WINDOW, GATHER_WINDOW))


def _scatter_rows(src, idx, out_rows):
    n = idx.shape[0]
    src_blocks = src.shape[0] // GATHER_WINDOW
    mesh = plsc.VectorSubcoreMesh(core_axis_name="core", subcore_axis_name="subcore")

    @pl.kernel(out_type=jax.ShapeDtypeStruct((out_rows, src.shape[1]), src.dtype), mesh=mesh,
               scratch_types=[])
    def scatter(src_hbm, idx_hbm, out_hbm):
        def body(src_vmem, idx_vmem):
            pltpu.sync_copy(src_vmem, out_hbm.at[idx_vmem.at[0]])

        pltpu.emit_pipeline(
            body,
            grid=(n // GATHER_WINDOW,),
            in_specs=[pl.BlockSpec((GATHER_WINDOW, src.shape[1]),
                                   index_map=lambda i: (i % src_blocks, 0)),
                      pl.BlockSpec((1, GATHER_WINDOW), index_map=lambda i: (i, 0))],
            out_specs=[],
            core_axis_name=("core", "subcore"),
            dimension_semantics=(pltpu.PARALLEL,),
        )(src_hbm, idx_hbm)

    return scatter(src, idx.reshape(n // GATHER_WINDOW, GATHER_WINDOW))


def _slab_row_index(rows):
    sub = jnp.arange(SLAB[0], dtype=jnp.int32)
    return (rows[:, None] * SLAB[0] + sub[None, :]).reshape(-1)


def _scatter_slabs(slabs, rows, out_rows):
    out = _scatter_rows(slabs.reshape(-1, SLAB[1]), _slab_row_index(rows), out_rows * SLAB[0])
    return out.reshape((out_rows,) + SLAB)


def _gather_slabs(slabs, rows):
    out = _gather_rows(slabs.reshape(-1, SLAB[1]), _slab_row_index(rows))
    return out.reshape((rows.shape[0],) + SLAB)


def _gmm_kernel(te_ref, xs_ref, wg_ref, wu_ref, wd_ref, o_ref, acc_ref, xb_ref, *, n_tiles):
    i = pl.program_id(0)
    f = pl.program_id(1)

    @pl.when(f == 0)
    def _():
        acc_ref[...] = jnp.zeros_like(acc_ref)
        row = lax.broadcasted_iota(jnp.int32, (xb_ref.shape[0], 1), 0)
        live = row < te_ref[n_tiles + 1 + i]
        xb_ref[...] = jnp.where(live, _from_slabs(xs_ref), 0.0).astype(BF16)

    @pl.when(i < te_ref[n_tiles])
    def _():
        xb = xb_ref[...]
        hid = _silu(_dot(xb, wg_ref[...])) * _dot(xb, wu_ref[...])
        acc_ref[...] += _dot(hid.astype(BF16), wd_ref[...])

    @pl.when(f == pl.num_programs(1) - 1)
    def _():
        _to_slabs(acc_ref[...], o_ref)


def _grouped_swiglu(tile_expert, xs, wg, wu, wd, tm, tf):
    m = xs.shape[0]
    d_ff = wg.shape[2]
    n_tiles = m // tm
    grid_spec = pltpu.PrefetchScalarGridSpec(
        num_scalar_prefetch=1,
        grid=(n_tiles, d_ff // tf),
        in_specs=[pl.BlockSpec((tm,) + SLAB, lambda i, f, te: (i, 0, 0)),
                  pl.BlockSpec((None, D_MODEL, tf), lambda i, f, te: (te[i], 0, f)),
                  pl.BlockSpec((None, D_MODEL, tf), lambda i, f, te: (te[i], 0, f)),
                  pl.BlockSpec((None, tf, D_MODEL), lambda i, f, te: (te[i], f, 0))],
        out_specs=pl.BlockSpec((tm,) + SLAB, lambda i, f, te: (i, 0, 0)),
        scratch_shapes=[pltpu.VMEM((tm, D_MODEL), F32), pltpu.VMEM((tm, D_MODEL), BF16)])
    return pl.pallas_call(
        functools.partial(_gmm_kernel, n_tiles=n_tiles),
        grid_spec=grid_spec,
        out_shape=jax.ShapeDtypeStruct((m,) + SLAB, F32),
        compiler_params=_cparams("parallel", "arbitrary"),
        name="grouped_swiglu",
    )(tile_expert, xs, wg, wu, wd)


def _combine_kernel(x_ref, y0_ref, y1_ref, meta_ref, p_ref, pg_ref, pp_ref, lg_ref, lb_ref, *rest):
    o_ref = rest[-1]
    x = x_ref[...]
    meta = meta_ref[...]
    g0 = meta[:, META_G0:META_G0 + 1]
    g1 = meta[:, META_G1:META_G1 + 1]
    ple = (jax.nn.sigmoid(_dot(x.astype(BF16), pg_ref[...]))
           * _dot(p_ref[...].astype(BF16), pp_ref[...]))
    o = DEEPNORM_ALPHA * x + g0 * _from_slabs(y0_ref) + g1 * _from_slabs(y1_ref) + ple
    o_ref[...] = _layer_norm(o, lg_ref[...], lb_ref[...])


def _combine_ln2(x2, gathered, meta, p2, pg, pp, lg, lb, tm, first_block, out_prev):
    n = x2.shape[0]
    ple_dim = p2.shape[1]
    nb = gathered.shape[0] // 2 // tm

    def tok(width):
        return pl.BlockSpec((tm, width), lambda i: (i + first_block, 0))

    def const(shape):
        return pl.BlockSpec(shape, lambda i: (0, 0))

    operands = [x2, gathered, gathered, meta, p2, pg, pp, lg, lb]
    in_specs = [tok(D_MODEL), pl.BlockSpec((tm,) + SLAB, lambda i: (i, 0, 0)),
                pl.BlockSpec((tm,) + SLAB, lambda i: (i + nb, 0, 0)),
                tok(128), tok(ple_dim), const((D_MODEL, D_MODEL)), const((ple_dim, D_MODEL)),
                const((1, D_MODEL)), const((1, D_MODEL))]
    aliases = {}
    if out_prev is not None:
        operands.append(out_prev)
        in_specs.append(pl.BlockSpec(memory_space=pl.ANY))
        aliases = {9: 0}
    return pl.pallas_call(
        _combine_kernel,
        grid=(nb,),
        in_specs=in_specs,
        out_specs=tok(D_MODEL),
        out_shape=jax.ShapeDtypeStruct((n, D_MODEL), F32),
        input_output_aliases=aliases,
        compiler_params=_cparams("parallel"),
        name="combine_ple_ln2",
    )(*operands)


def _moe(x2, w_router, wg, wu, wd, p2, pg, pp, lg, lb, tm, tg, tf):
    n = x2.shape[0]
    wr = jnp.pad(w_router, ((0, 0), (0, 128 - N_EXPERTS)))
    meta, counts, x_slabs = _router(x2, wr, tm)
    cnt = counts[0, :N_EXPERTS].astype(jnp.int32)
    padded = ((cnt + tg - 1) // tg) * tg
    ends = jnp.cumsum(padded)
    starts = ends - padded
    experts = jnp.arange(N_EXPERTS, dtype=jnp.int32)

    def dest(e_lane, rank_lane):
        e = meta[:, e_lane].astype(jnp.int32)
        start = jnp.sum(jnp.where(e[:, None] == experts[None, :], starts[None, :], 0), axis=1)
        return start + meta[:, rank_lane].astype(jnp.int32)

    dest01 = jnp.concatenate([dest(META_E0, META_RANK0), dest(META_E1, META_RANK1)])
    m = 2 * n + N_EXPERTS * tg
    n_tiles = m // tg
    tile_start = jnp.arange(n_tiles, dtype=jnp.int32) * tg
    tile_expert = jnp.minimum(
        jnp.sum((ends[None, :] <= tile_start[:, None]).astype(jnp.int32), axis=1), N_EXPERTS - 1)
    onehot = tile_expert[:, None] == experts[None, :]
    tile_end = jnp.sum(jnp.where(onehot, (starts + cnt)[None, :], 0), axis=1)
    live_rows = jnp.clip(tile_end - tile_start, 0, tg)
    prefetch = jnp.concatenate([tile_expert, (ends[-1] // tg)[None], live_rows]).astype(jnp.int32)
    xs = _scatter_slabs(x_slabs, dest01, m)
    ys = _grouped_swiglu(prefetch, xs, wg, wu, wd, tg, tf)
    out = None
    half = n // 2
    for part in range(2):
        lo = part * half
        rows = jnp.concatenate([dest01[lo:lo + half], dest01[n + lo:n + lo + half]])
        out = _combine_ln2(x2, _gather_slabs(ys, rows), meta, p2, pg, pp, lg, lb, tm,
                           lo // tm, out)
    return out


def _ffn_kernel(x_ref, wg_ref, wu_ref, wd_ref, p_ref, pg_ref, pp_ref, lg_ref, lb_ref,
                o_ref, acc_ref, xb_ref):
    f = pl.program_id(1)

    @pl.when(f == 0)
    def _():
        acc_ref[...] = jnp.zeros_like(acc_ref)
        xb_ref[...] = x_ref[...].astype(BF16)

    xb = xb_ref[...]
    hid = _silu(_dot(xb, wg_ref[...])) * _dot(xb, wu_ref[...])
    acc_ref[...] += _dot(hid.astype(BF16), wd_ref[...])

    @pl.when(f == pl.num_programs(1) - 1)
    def _():
        ple = (jax.nn.sigmoid(_dot(xb, pg_ref[...]))
               * _dot(p_ref[...].astype(BF16), pp_ref[...]))
        o = DEEPNORM_ALPHA * x_ref[...] + acc_ref[...] + ple
        o_ref[...] = _layer_norm(o, lg_ref[...], lb_ref[...])


def _ffn(x2, wg, wu, wd, p2, pg, pp, lg, lb, tm, tf):
    n = x2.shape[0]
    d_ff = wg.shape[1]
    ple_dim = p2.shape[1]

    def tok(width):
        return pl.BlockSpec((tm, width), lambda i, f: (i, 0))

    def const(shape):
        return pl.BlockSpec(shape, lambda i, f: (0, 0))

    return pl.pallas_call(
        _ffn_kernel,
        grid=(n // tm, d_ff // tf),
        in_specs=[tok(D_MODEL),
                  pl.BlockSpec((D_MODEL, tf), lambda i, f: (0, f)),
                  pl.BlockSpec((D_MODEL, tf), lambda i, f: (0, f)),
                  pl.BlockSpec((tf, D_MODEL), lambda i, f: (f, 0)),
                  tok(ple_dim), const((D_MODEL, D_MODEL)), const((ple_dim, D_MODEL)),
                  const((1, D_MODEL)), const((1, D_MODEL))],
        out_specs=tok(D_MODEL),
        out_shape=jax.ShapeDtypeStruct((n, D_MODEL), F32),
        scratch_shapes=[pltpu.VMEM((tm, D_MODEL), F32), pltpu.VMEM((tm, D_MODEL), BF16)],
        compiler_params=_cparams("parallel", "arbitrary"),
        name="ffn_ple_ln2",
    )(x2, wg, wu, wd, p2, pg, pp, lg, lb)


def _prep_w_in(w):
    a, b, c, d, g = jnp.split(w, [768, 1280, 2048, 3336], axis=1)
    z, xs, bc, dt = jnp.split(d, [512, 1024, 1280], axis=1)

    def scale_q(t, scale):
        return jnp.concatenate([t[:, :MIX_W] * scale, t[:, MIX_W:]], axis=1)

    qk_scale = HEAD_DIM ** -0.5
    pad = jnp.zeros((w.shape[0], IN_COLS - (COL_DT + 8)), w.dtype)
    mixers = jnp.concatenate([scale_q(a, qk_scale * LOG2_E), b, scale_q(c, qk_scale), z, xs, bc, dt,
                              pad], axis=1)
    return mixers.astype(BF16), g.astype(BF16)


def _row(v):
    return v.reshape(1, -1).astype(F32)


def _per_head(v):
    return _row(jnp.repeat(v, HEAD_DIM))


def kernel(x, p, w_in, w_br_a, w_br_b, w_br_c, w_br_d, w_out, sg_ln_g, sg_ln_b, sg_w, sg_b,
           ca_rel_bias, ssd_conv_w, ssd_conv_b, ssd_dt_bias, ssd_a_log, ssd_d, ssd_norm_g,
           ln1_g, ln1_b, ffn_w_gate, ffn_w_up, ffn_w_down, moe_router, moe_w_gate, moe_w_up,
           moe_w_down, ple_w_gate, ple_w_proj, ln2_g, ln2_b):
    bsz, seq, _ = x.shape
    n = bsz * seq
    x2 = x.reshape(n, D_MODEL)
    tm_proj = min(1024, n)
    tm = min(512, n)
    sb_blk = min(256, seq)
    ssd_q = min(256, seq)
    sg_tb = min(1024, seq)

    for i in range(DEPTH):
        w_mix, w_gate = _prep_w_in(w_in[i])
        h = _inproj(x2, w_mix, tm_proj, IN_COLS)
        ya = _stick_breaking(h, bsz, seq, sb_blk)
        sg_bias = jnp.repeat(jnp.transpose(sg_b[i]), HEAD_DIM, axis=1).astype(F32)
        yb = _spatial_gating(h, _row(sg_ln_g[i]), _row(sg_ln_b[i]), sg_w[i], sg_bias, sg_tb)
        yc = _band_attention(h, _band_bias(ca_rel_bias[i]), seq)
        cw, cb = ssd_conv_w[i], ssd_conv_b[i]
        yd = _ssd(h, cw[:, :SSD_INNER], _row(cb[:SSD_INNER]), cw[:, SSD_INNER:],
                  _row(cb[SSD_INNER:]), _per_head(ssd_dt_bias[i]), _per_head(ssd_a_log[i]),
                  _per_head(ssd_d[i]), _row(ssd_norm_g[i]), bsz, seq, ssd_q)
        x2 = _merge(x2, w_gate, ya, yb, yc, yd, w_br_a[i].astype(BF16), w_br_b[i].astype(BF16),
                    w_br_c[i].astype(BF16), w_br_d[i].astype(BF16), w_out[i].astype(BF16),
                    _row(ln1_g[i]), _row(ln1_b[i]), tm_proj)
        p2 = p[i].reshape(n, -1)
        pg, pp = ple_w_gate[i].astype(BF16), ple_w_proj[i].astype(BF16)
        j = i // 2
        if i % 2 == 0:
            x2 = _ffn(x2, ffn_w_gate[j].astype(BF16), ffn_w_up[j].astype(BF16),
                      ffn_w_down[j].astype(BF16), p2, pg, pp, _row(ln2_g[i]), _row(ln2_b[i]),
                      tm, 2816)
        else:
            x2 = _moe(x2, moe_router[j], moe_w_gate[j].astype(BF16), moe_w_up[j].astype(BF16),
                      moe_w_down[j].astype(BF16), p2, pg, pp, _row(ln2_g[i]), _row(ln2_b[i]),
                      tm, min(1024, n), 1792)
    return x2.reshape(bsz, seq, D_MODEL)
```

```python
import functools

import jax
import jax.numpy as jnp
from jax import lax
from jax.experimental import pallas as pl
from jax.experimental.pallas import tpu as pltpu
from jax.experimental.pallas import tpu_sc as plsc

F32 = jnp.float32
BF16 = jnp.bfloat16

D_MODEL = 1024
DEPTH = 2
LN_EPS = 1e-5
DEEPNORM_ALPHA = (2 * DEPTH) ** 0.25
HEAD_DIM = 64
MIX_W = 256
SG_CHUNK = 128
BAND_CHUNK = 64
BAND_LEFT = 8
BAND_REL_MAX = 256
SSD_INNER = 512
SSD_HEADS = 8
SSD_BC = 256
N_EXPERTS = 8
NEG_BIG = -1e30
LOG2_E = 1.4426950408889634
EXP2_UNDERFLOW = -160.0

COL_A = 0
COL_B = 768
COL_C = 1280
COL_Z = 2048
COL_XS = 2560
COL_BC = 3072
COL_DT = 3328
IN_COLS = 3584

VMEM_LIMIT = 56 * 1024 * 1024


def _cparams(*sem):
    return pltpu.CompilerParams(dimension_semantics=sem, vmem_limit_bytes=VMEM_LIMIT)


def _nt_dot(a, b):
    return lax.dot_general(a, b, (((1,), (1,)), ((), ())), preferred_element_type=F32)


def _dot(a, b):
    return jnp.dot(a, b, preferred_element_type=F32)


def _softplus(x):
    return jnp.maximum(x, 0.0) + jnp.log(1.0 + jnp.exp(-jnp.abs(x)))


def _softplus2(x):
    return jnp.maximum(x, 0.0) + jnp.log2(1.0 + jnp.exp2(-jnp.abs(x)))


def _silu(x):
    return x * jax.nn.sigmoid(x)


def _split2(x):
    hi = x.astype(BF16)
    lo = (x - hi.astype(F32)).astype(BF16)
    return hi, lo


def _split3(x):
    hi = x.astype(BF16)
    r = x - hi.astype(F32)
    mid = r.astype(BF16)
    lo = (r - mid.astype(F32)).astype(BF16)
    return hi, mid, lo


def _layer_norm(x, g, b):
    mu = jnp.mean(x, axis=-1, keepdims=True)
    xc = x - mu
    var = jnp.mean(xc * xc, axis=-1, keepdims=True)
    return xc * lax.rsqrt(var + LN_EPS) * g + b


def _inproj_kernel(x_ref, w_ref, o_ref):
    o_ref[...] = _dot(x_ref[...].astype(BF16), w_ref[...]).astype(o_ref.dtype)


def _inproj(x2, w, tm, tn):
    n, d = x2.shape
    nc = w.shape[1]
    return pl.pallas_call(
        _inproj_kernel,
        grid=(n // tm, nc // tn),
        in_specs=[pl.BlockSpec((tm, d), lambda i, j: (i, 0)),
                  pl.BlockSpec((d, tn), lambda i, j: (0, j))],
        out_specs=pl.BlockSpec((tm, tn), lambda i, j: (i, j)),
        out_shape=jax.ShapeDtypeStruct((n, nc), BF16),
        compiler_params=_cparams("parallel", "arbitrary"),
        name="inproj",
    )(x2, w)


def _sb_kernel(q_ref, k_ref, v_ref, o_ref, acc_ref, run_ref, kmax_ref, *, blk, seq):
    qi = pl.program_id(1)
    lane = lax.broadcasted_iota(jnp.int32, (1, MIX_W), 1)
    head_mask = [(lane // HEAD_DIM) == h for h in range(4)]

    def head_sq_norms(t):
        tf = t.astype(F32)
        sq = tf * tf
        return [jnp.sum(jnp.where(head_mask[h], sq, 0.0), axis=-1, keepdims=True)
                for h in range(4)]

    @pl.when(qi == 0)
    def _():
        kmax_ref[...] = jnp.zeros_like(kmax_ref)

        def scan(c, carry):
            norms = head_sq_norms(k_ref[pl.ds(pl.multiple_of(c * blk, blk), blk), :])
            for h in range(4):
                kmax_ref[h] = jnp.maximum(kmax_ref[h], jnp.max(norms[h], axis=0, keepdims=True))
            return carry

        lax.fori_loop(0, seq // blk, scan, 0)

    q = q_ref[...]
    q_heads = [jnp.where(head_mask[h], q, jnp.zeros_like(q)) for h in range(4)]
    q_norms = head_sq_norms(q)
    z_bound = [jnp.sqrt(q_norms[h] * kmax_ref[h]) * 1.001 + 1e-3 for h in range(4)]
    row = lax.broadcasted_iota(jnp.int32, (blk, blk), 0)
    col = lax.broadcasted_iota(jnp.int32, (blk, blk), 1)
    below_diag = col < row
    suffix = jnp.where(row >= col, 1.0, 0.0).astype(BF16)
    suffix2 = jnp.concatenate([suffix, suffix], axis=0)

    acc_ref[...] = jnp.zeros_like(acc_ref)
    run_ref[...] = jnp.zeros_like(run_ref)

    half = blk // 2

    def process(j, masked, rows):
        kb = qi - j
        start = pl.multiple_of(kb * blk, blk)
        k_blk = k_ref[pl.ds(start, blk), :]
        v_blk = v_ref[pl.ds(start, blk), :]
        n_rows = rows.stop - rows.start
        weights = []
        slack = [None, None]
        for h in range(4):
            z = _nt_dot(q_heads[h][rows], k_blk)
            sp = _softplus2(z)
            if masked:
                sp = jnp.where(below_diag[rows], sp, 0.0)
            cs = _dot(jnp.concatenate(_split2(sp), axis=1), suffix2)
            run = run_ref[h, rows, :]
            arg = z - cs - run
            if masked:
                arg = jnp.where(below_diag[rows], arg, NEG_BIG)
            w = jnp.exp2(arg)
            run = run + cs[:, 0:1]
            run_ref[h, rows, :] = run
            weights.append(w.astype(BF16))
            gap = z_bound[h][rows] - run
            for part in range(n_rows // half):
                part_slack = jnp.max(gap[part * half:(part + 1) * half])
                slack[part] = (part_slack if slack[part] is None
                               else jnp.maximum(slack[part], part_slack))
        wcat = jnp.concatenate(weights, axis=1)
        vcat = jnp.concatenate(
            [jnp.where(head_mask[h], v_blk, jnp.zeros_like(v_blk)) for h in range(4)], axis=0)
        acc_ref[rows, :] += _dot(wcat, vcat)
        return slack

    all_rows = slice(0, blk)
    top_rows = slice(0, half)

    def cond(carry):
        j, top_live, bottom_live = carry
        return jnp.logical_and(j <= qi, jnp.logical_or(top_live, bottom_live))

    def body(carry):
        j, _, bottom_live = carry

        def both():
            top, bottom = process(j, False, all_rows)
            return top > EXP2_UNDERFLOW, bottom > EXP2_UNDERFLOW

        def top_only():
            top, _ = process(j, False, top_rows)
            return top > EXP2_UNDERFLOW, jnp.bool_(False)

        top_live, bottom_live = lax.cond(bottom_live, both, top_only)
        return j + 1, top_live, bottom_live

    top0, bottom0 = process(0, True, all_rows)
    lax.while_loop(cond, body, (jnp.int32(1), top0 > EXP2_UNDERFLOW, bottom0 > EXP2_UNDERFLOW))
    o_ref[...] = acc_ref[...].astype(o_ref.dtype)


def _stick_breaking(h, bsz, seq, blk):
    n = bsz * seq
    nq = seq // blk
    cq, ck, cv = (COL_A // MIX_W + i for i in range(3))
    return pl.pallas_call(
        functools.partial(_sb_kernel, blk=blk, seq=seq),
        grid=(bsz, nq),
        in_specs=[pl.BlockSpec((blk, MIX_W), lambda b, i: (b * nq + i, cq)),
                  pl.BlockSpec((seq, MIX_W), lambda b, i: (b, ck)),
                  pl.BlockSpec((seq, MIX_W), lambda b, i: (b, cv))],
        out_specs=pl.BlockSpec((blk, MIX_W), lambda b, i: (b * nq + i, 0)),
        out_shape=jax.ShapeDtypeStruct((n, MIX_W), BF16),
        scratch_shapes=[pltpu.VMEM((blk, MIX_W), F32), pltpu.VMEM((4, blk, 1), F32),
                        pltpu.VMEM((4, 1, 1), F32)],
        compiler_params=_cparams("parallel", "arbitrary"),
        name="stick_breaking",
    )(h, h, h)


def _gelu_tanh(x):
    return 0.5 * x * (1.0 + jnp.tanh(0.7978845608028654 * (x + 0.044715 * (x * x * x))))


def _sg_kernel(u_ref, v_ref, g_ref, b_ref, w_ref, bias_ref, o_ref, *, tb):
    u = _gelu_tanh(u_ref[...].astype(F32))
    v = _gelu_tanh(v_ref[...].astype(F32))
    vn = _layer_norm(v, g_ref[...], b_ref[...]).astype(BF16)
    lane = lax.broadcasted_iota(jnp.int32, (1, MIX_W), 1)
    group_mask = [(lane // HEAD_DIM) == g for g in range(4)]
    row = lax.broadcasted_iota(jnp.int32, (SG_CHUNK, SG_CHUNK), 0)
    col = lax.broadcasted_iota(jnp.int32, (SG_CHUNK, SG_CHUNK), 1)
    causal = col <= row
    wcat = jnp.concatenate(
        [jnp.where(causal, w_ref[g], 0.0).astype(BF16) for g in range(4)], axis=1)
    bias = bias_ref[...]
    for c in range(tb // SG_CHUNK):
        sl = slice(c * SG_CHUNK, (c + 1) * SG_CHUNK)
        vc = vn[sl, :]
        vstack = jnp.concatenate(
            [jnp.where(group_mask[g], vc, jnp.zeros_like(vc)) for g in range(4)], axis=0)
        mixed = _dot(wcat, vstack) + bias
        o_ref[sl, :] = (u[sl, :] * mixed).astype(o_ref.dtype)


def _spatial_gating(h, ln_g, ln_b, w_s, bias_full, tb):
    n = h.shape[0]
    cu, cv = COL_B // MIX_W, COL_B // MIX_W + 1
    return pl.pallas_call(
        functools.partial(_sg_kernel, tb=tb),
        grid=(n // tb,),
        in_specs=[pl.BlockSpec((tb, MIX_W), lambda i: (i, cu)),
                  pl.BlockSpec((tb, MIX_W), lambda i: (i, cv)),
                  pl.BlockSpec((1, MIX_W), lambda i: (0, 0)),
                  pl.BlockSpec((1, MIX_W), lambda i: (0, 0)),
                  pl.BlockSpec((4, SG_CHUNK, SG_CHUNK), lambda i: (0, 0, 0)),
                  pl.BlockSpec((SG_CHUNK, MIX_W), lambda i: (0, 0))],
        out_specs=pl.BlockSpec((tb, MIX_W), lambda i: (i, 0)),
        out_shape=jax.ShapeDtypeStruct((n, MIX_W), BF16),
        compiler_params=_cparams("parallel"),
        name="spatial_gating",
    )(h, h, ln_g, ln_b, w_s, bias_full)


BAND_TQ = 256
BAND_PREV = BAND_LEFT * BAND_CHUNK
BAND_WIN = BAND_PREV + BAND_TQ


def _band_kernel(q_ref, k2_ref, k1_ref, k0_ref, v2_ref, v1_ref, v0_ref, bias_ref, o_ref, *, nblk):
    bi = pl.program_id(0) % nblk
    lane = lax.broadcasted_iota(jnp.int32, (1, MIX_W), 1)
    q = q_ref[...]
    kcat = jnp.concatenate([k2_ref[...], k1_ref[...], k0_ref[...]], axis=0)
    vcat = jnp.concatenate([v2_ref[...], v1_ref[...], v0_ref[...]], axis=0)
    col = lax.broadcasted_iota(jnp.int32, (1, BAND_WIN), 1)
    in_seq = col >= (2 - jnp.minimum(bi, 2)) * BAND_TQ
    out = jnp.zeros((BAND_TQ, MIX_W), F32)
    for h in range(4):
        hm = (lane // HEAD_DIM) == h
        s = _nt_dot(jnp.where(hm, q, jnp.zeros_like(q)), kcat) + bias_ref[h]
        s = jnp.where(in_seq, s, NEG_BIG)
        m = jnp.max(s, axis=-1, keepdims=True)
        p = jnp.exp(s - m)
        l = jnp.sum(p, axis=-1, keepdims=True)
        o = _dot(p.astype(BF16), vcat) / l
        out = jnp.where(hm, o, out)
    o_ref[...] = out.astype(o_ref.dtype)


def _band_attention(h, bias_full, seq):
    n = h.shape[0]
    nblk = seq // BAND_TQ
    cq, ck, cv = (COL_C // MIX_W + i for i in range(3))

    def prev(i, d):
        return i - jnp.minimum(i % nblk, d)

    def spec(c, d):
        return pl.BlockSpec((BAND_TQ, MIX_W), lambda i: (prev(i, d), c))

    return pl.pallas_call(
        functools.partial(_band_kernel, nblk=nblk),
        grid=(n // BAND_TQ,),
        in_specs=[spec(cq, 0), spec(ck, 2), spec(ck, 1), spec(ck, 0),
                  spec(cv, 2), spec(cv, 1), spec(cv, 0),
                  pl.BlockSpec((4, BAND_TQ, BAND_WIN), lambda i: (0, 0, 0))],
        out_specs=pl.BlockSpec((BAND_TQ, MIX_W), lambda i: (i, 0)),
        out_shape=jax.ShapeDtypeStruct((n, MIX_W), BF16),
        compiler_params=_cparams("parallel"),
        name="band_attention",
    )(h, h, h, h, h, h, h, bias_full)


def _band_bias(rel_bias):
    period = 1024
    u = jnp.arange(period)
    rel = jnp.where(u <= BAND_WIN, BAND_PREV - u, BAND_PREV + period - u)
    row0 = rel_bias.astype(F32)[:, jnp.clip(rel, -(BAND_CHUNK - 1), BAND_REL_MAX) + (BAND_CHUNK - 1)]
    skew = jnp.tile(row0, (1, BAND_TQ))[:, :BAND_TQ * (period - 1)]
    bias = skew.reshape(4, BAND_TQ, period - 1)[:, :, :BAND_WIN]
    tc = (jnp.arange(BAND_TQ)[:, None] + BAND_PREV) // BAND_CHUNK
    sc = jnp.arange(BAND_WIN)[None, :] // BAND_CHUNK
    in_band = jnp.logical_and(sc <= tc, sc >= tc - BAND_LEFT)
    return jnp.where(in_band[None], bias, NEG_BIG)


def _ssd_kernel(z_ref, xs_ref, bc_ref, dt_ref, cwx_ref, cbx_ref, cwb_ref, cbb_ref, dtb_ref,
                alog_ref, dsk_ref, ng_ref, o_ref, xpad_ref, bpad_ref, state_ref, expand_ref,
                pick_ref, tri_ref, eye_ref, *, q):
    c = pl.program_id(1)
    row = lax.broadcasted_iota(jnp.int32, (q, q), 0)
    col = lax.broadcasted_iota(jnp.int32, (q, q), 1)
    causal = col <= row

    @pl.when(c == 0)
    def _():
        xpad_ref[0:8, :] = jnp.zeros((8, SSD_INNER), F32)
        bpad_ref[0:8, :] = jnp.zeros((8, SSD_BC), F32)
        state_ref[...] = jnp.zeros_like(state_ref)
        r128 = lax.broadcasted_iota(jnp.int32, (128, SSD_INNER), 0)
        c512 = lax.broadcasted_iota(jnp.int32, (128, SSD_INNER), 1)
        expand_ref[...] = jnp.where(c512 // HEAD_DIM == r128, 1.0, 0.0).astype(BF16)
        pick_ref[...] = jnp.where(c512 == r128 * HEAD_DIM, 1.0, 0.0).astype(BF16)
        tri_ref[...] = jnp.where(causal, 1.0, 0.0).astype(BF16)
        eye_ref[...] = jnp.where(lax.broadcasted_iota(jnp.int32, (128, 128), 0)
                                 == lax.broadcasted_iota(jnp.int32, (128, 128), 1),
                                 1.0, 0.0).astype(BF16)

    xpad_ref[8:8 + q, :] = xs_ref[...].astype(F32)
    bpad_ref[8:8 + q, :] = bc_ref[...].astype(F32)

    def conv_silu(pad_ref, w_ref, b_ref):
        xp = pad_ref[...]
        acc = b_ref[...] + w_ref[3:4, :] * xp[8:8 + q, :]
        for d in (1, 2, 3):
            acc = acc + w_ref[3 - d:4 - d, :] * pltpu.roll(xp, d, axis=0)[8:8 + q, :]
        return _silu(acc)

    xs = conv_silu(xpad_ref, cwx_ref, cbx_ref)
    bc = conv_silu(bpad_ref, cwb_ref, cbb_ref)
    xpad_ref[0:8, :] = xpad_ref[q:q + 8, :]
    bpad_ref[0:8, :] = bpad_ref[q:q + 8, :]
    bm = bc[:, 0:128].astype(BF16)
    cm = bc[:, 128:256].astype(BF16)

    pick = pick_ref[...]
    tri = tri_ref[...]
    dt = _softplus(_dot(dt_ref[...], expand_ref[...]) + dtb_ref[...])
    da = dt * (-jnp.exp(alog_ref[...]))
    da_hi, da_lo = _split2(da)
    acs = _dot(tri, da_hi) + _dot(tri, da_lo)
    a_hi, a_mid, a_lo = _split3(acs)
    acs_t = _nt_dot(pick, a_hi) + _nt_dot(pick, a_mid) + _nt_dot(pick, a_lo)
    xdt = xs * dt

    bm_t = _nt_dot(eye_ref[...], bm).astype(BF16)
    lane128 = lax.broadcasted_iota(jnp.int32, (1, 128), 1)
    lane256 = lax.broadcasted_iota(jnp.int32, (1, 256), 1)

    y_groups = []
    for g in range(2):
        gm = (lane128 // HEAD_DIM) == g
        cb = _nt_dot(jnp.where(gm, cm, jnp.zeros_like(cm)), bm)
        xg = xdt[:, g * 256:(g + 1) * 256].astype(BF16)
        ms, xstack = [], []
        for hh in range(4):
            head = g * 4 + hh
            seg = acs[:, head * HEAD_DIM:head * HEAD_DIM + 1] - acs_t[head:head + 1, :]
            decay = jnp.exp(jnp.where(causal, seg, NEG_BIG))
            ms.append((cb * decay).astype(BF16))
            xstack.append(jnp.where((lane256 // HEAD_DIM) == hh, xg, jnp.zeros_like(xg)))
        y_groups.append(_dot(jnp.concatenate(ms, axis=1), jnp.concatenate(xstack, axis=0)))
    y_diag = jnp.concatenate(y_groups, axis=1)

    state = state_ref[...]
    y_off = _dot(cm, state.astype(BF16)) * jnp.exp(acs)
    acs_last = acs[q - 1:q, :]
    xw = (xdt * jnp.exp(acs_last - acs)).astype(BF16)
    keep = (lax.broadcasted_iota(jnp.int32, (128, SSD_INNER), 0) // HEAD_DIM
            == lax.broadcasted_iota(jnp.int32, (128, SSD_INNER), 1) // 256)
    state_ref[...] = jnp.where(keep, state * jnp.exp(acs_last) + _dot(bm_t, xw), 0.0)

    y = y_diag + y_off + xs * dsk_ref[...]
    y = y * _silu(z_ref[...].astype(F32))
    outs = []
    for g in range(2):
        yg = y[:, g * 256:(g + 1) * 256]
        outs.append(yg * lax.rsqrt(jnp.mean(yg * yg, axis=-1, keepdims=True) + LN_EPS))
    o_ref[...] = (jnp.concatenate(outs, axis=1) * ng_ref[...]).astype(o_ref.dtype)


def _ssd(h, conv_wx, conv_bx, conv_wb, conv_bb, dt_bias_e, a_log_e, d_e, norm_g, bsz, seq, q):
    n = bsz * seq
    nc = seq // q

    def tok(width, colblk):
        return pl.BlockSpec((q, width), lambda b, c: (b * nc + c, colblk))

    def const(shape):
        return pl.BlockSpec(shape, lambda b, c: (0, 0))

    return pl.pallas_call(
        functools.partial(_ssd_kernel, q=q),
        grid=(bsz, nc),
        in_specs=[tok(SSD_INNER, COL_Z // SSD_INNER), tok(SSD_INNER, COL_XS // SSD_INNER),
                  tok(SSD_BC, COL_BC // SSD_BC), tok(128, COL_DT // 128),
                  const((4, SSD_INNER)), const((1, SSD_INNER)),
                  const((4, SSD_BC)), const((1, SSD_BC)),
                  const((1, SSD_INNER)), const((1, SSD_INNER)), const((1, SSD_INNER)),
                  const((1, SSD_INNER))],
        out_specs=pl.BlockSpec((q, SSD_INNER), lambda b, c: (b * nc + c, 0)),
        out_shape=jax.ShapeDtypeStruct((n, SSD_INNER), BF16),
        scratch_shapes=[pltpu.VMEM((q + 8, SSD_INNER), F32), pltpu.VMEM((q + 8, SSD_BC), F32),
                        pltpu.VMEM((128, SSD_INNER), F32), pltpu.VMEM((128, SSD_INNER), BF16),
                        pltpu.VMEM((128, SSD_INNER), BF16), pltpu.VMEM((q, q), BF16),
                        pltpu.VMEM((128, 128), BF16)],
        compiler_params=_cparams("parallel", "arbitrary"),
        name="ssd",
    )(h, h, h, h, conv_wx, conv_bx, conv_wb, conv_bb, dt_bias_e, a_log_e, d_e, norm_g)


def _merge_kernel(x_ref, wg_ref, ya_ref, yb_ref, yc_ref, yd_ref, wa_ref, wb_ref, wc_ref, wd_ref,
                  wo_ref, lg_ref, lb_ref, o_ref):
    x = x_ref[...]
    xb = x.astype(BF16)
    merged = None
    for i, (y_ref, w_ref) in enumerate(((ya_ref, wa_ref), (yb_ref, wb_ref), (yc_ref, wc_ref),
                                        (yd_ref, wd_ref))):
        gate = jax.nn.sigmoid(_dot(xb, wg_ref[:, i * D_MODEL:(i + 1) * D_MODEL]))
        term = gate * _dot(y_ref[...], w_ref[...])
        merged = term if merged is None else merged + term
    o = DEEPNORM_ALPHA * x + _dot(merged.astype(BF16), wo_ref[...])
    o_ref[...] = _layer_norm(o, lg_ref[...], lb_ref[...])


def _merge(x2, w_gate, ya, yb, yc, yd, wa, wb, wc, wd, wo, lg, lb, tm):
    n = x2.shape[0]

    def tok(width):
        return pl.BlockSpec((tm, width), lambda i: (i, 0))

    def const(shape):
        return pl.BlockSpec(shape, lambda i: (0, 0))

    return pl.pallas_call(
        _merge_kernel,
        grid=(n // tm,),
        in_specs=[tok(D_MODEL), const((D_MODEL, 4 * D_MODEL)), tok(MIX_W), tok(MIX_W), tok(MIX_W),
                  tok(SSD_INNER), const((MIX_W, D_MODEL)), const((MIX_W, D_MODEL)),
                  const((MIX_W, D_MODEL)), const((SSD_INNER, D_MODEL)),
                  const((D_MODEL, D_MODEL)), const((1, D_MODEL)), const((1, D_MODEL))],
        out_specs=tok(D_MODEL),
        out_shape=jax.ShapeDtypeStruct((n, D_MODEL), F32),
        compiler_params=_cparams("parallel"),
        name="merge_ln1",
    )(x2, w_gate, ya, yb, yc, yd, wa, wb, wc, wd, wo, lg, lb)


META_E0, META_E1, META_RANK0, META_RANK1, META_G0, META_G1 = range(6)

SLAB = (8, 128)


def _to_slabs(x, slab_ref):
    slab_ref[...] = x.reshape((x.shape[0],) + SLAB)


def _from_slabs(slab_ref):
    return slab_ref[...].reshape(slab_ref.shape[0], SLAB[0] * SLAB[1])


def _router_kernel(x_ref, w_ref, meta_ref, cnt_ref, slab_ref, base_ref, *, tm):
    @pl.when(pl.program_id(0) == 0)
    def _():
        base_ref[...] = jnp.zeros_like(base_ref)

    x = x_ref[...]
    _to_slabs(x, slab_ref)
    logits = jnp.dot(x, w_ref[...], preferred_element_type=F32,
                     precision=lax.Precision.HIGHEST)
    lane = lax.broadcasted_iota(jnp.int32, logits.shape, 1)
    lg = jnp.where(lane < N_EXPERTS, logits, -jnp.inf)
    m1 = jnp.max(lg, axis=-1, keepdims=True)
    i1 = jnp.min(jnp.where(lg == m1, lane, 128), axis=-1, keepdims=True)
    first = lane == i1
    lg2 = jnp.where(first, -jnp.inf, lg)
    m2 = jnp.max(lg2, axis=-1, keepdims=True)
    i2 = jnp.min(jnp.where(lg2 == m2, lane, 128), axis=-1, keepdims=True)
    second = lane == i2
    e2 = jnp.exp(m2 - m1)
    denom = 1.0 + e2
    sel = jnp.where(jnp.logical_or(first, second), 1.0, 0.0)
    row = lax.broadcasted_iota(jnp.int32, (tm, tm), 0)
    col = lax.broadcasted_iota(jnp.int32, (tm, tm), 1)
    strict_lower = jnp.where(col < row, 1.0, 0.0).astype(BF16)
    pos = base_ref[...] + _dot(strict_lower, sel.astype(BF16))
    rank0 = jnp.sum(jnp.where(first, pos, 0.0), axis=-1, keepdims=True)
    rank1 = jnp.sum(jnp.where(second, pos, 0.0), axis=-1, keepdims=True)
    base_ref[...] += jnp.sum(sel, axis=0, keepdims=True)
    cnt_ref[...] = jnp.broadcast_to(base_ref[...], cnt_ref.shape)
    fields = (i1.astype(F32), i2.astype(F32), rank0, rank1, 1.0 / denom, e2 / denom)
    meta = jnp.zeros(logits.shape, F32)
    for k, val in enumerate(fields):
        meta = jnp.where(lane == k, val, meta)
    meta_ref[...] = meta


def _router(x2, w_router_padded, tm):
    n = x2.shape[0]
    return pl.pallas_call(
        functools.partial(_router_kernel, tm=tm),
        grid=(n // tm,),
        in_specs=[pl.BlockSpec((tm, D_MODEL), lambda i: (i, 0)),
                  pl.BlockSpec((D_MODEL, 128), lambda i: (0, 0))],
        out_specs=[pl.BlockSpec((tm, 128), lambda i: (i, 0)),
                   pl.BlockSpec((8, 128), lambda i: (0, 0)),
                   pl.BlockSpec((tm,) + SLAB, lambda i: (i, 0, 0))],
        out_shape=[jax.ShapeDtypeStruct((n, 128), F32), jax.ShapeDtypeStruct((8, 128), F32),
                   jax.ShapeDtypeStruct((n,) + SLAB, F32)],
        scratch_shapes=[pltpu.VMEM((1, 128), F32)],
        compiler_params=_cparams("arbitrary"),
        name="router",
    )(x2, w_router_padded)


GATHER_WINDOW = 128


def _gather_rows(table, idx):
    n = idx.shape[0]
    mesh = plsc.VectorSubcoreMesh(core_axis_name="core", subcore_axis_name="subcore")

    @pl.kernel(out_type=jax.ShapeDtypeStruct((n, table.shape[1]), table.dtype), mesh=mesh)
    def gather(table_hbm, idx_hbm, out_hbm):
        def body(idx_vmem, out_vmem):
            pltpu.sync_copy(table_hbm.at[idx_vmem.at[0]], out_vmem)

        pltpu.emit_pipeline(
            body,
            grid=(n // GATHER_WINDOW,),
            in_specs=[pl.BlockSpec((1, GATHER_WINDOW), index_map=lambda i: (i, 0))],
            out_specs=[pl.BlockSpec((GATHER_WINDOW, table.shape[1]), index_map=lambda i: (i, 0))],
            core_axis_name=("core", "subcore"),
            dimension_semantics=(pltpu.PARALLEL,),
        )(idx_hbm, out_hbm)

    return gather(table, idx.reshape(n // GATHER_WINDOW, GATHER_WINDOW))


def _scatter_rows(src, idx, out_rows):
    n = idx.shape[0]
    src_blocks = src.shape[0] // GATHER_WINDOW
    mesh = plsc.VectorSubcoreMesh(core_axis_name="core", subcore_axis_name="subcore")

    @pl.kernel(out_type=jax.ShapeDtypeStruct((out_rows, src.shape[1]), src.dtype), mesh=mesh,
               scratch_types=[])
    def scatter(src_hbm, idx_hbm, out_hbm):
        def body(src_vmem, idx_vmem):
            pltpu.sync_copy(src_vmem, out_hbm.at[idx_vmem.at[0]])

        pltpu.emit_pipeline(
            body,
            grid=(n // GATHER_WINDOW,),
            in_specs=[pl.BlockSpec((GATHER_WINDOW, src.shape[1]),
                                   index_map=lambda i: (i % src_blocks, 0)),
                      pl.BlockSpec((1, GATHER_WINDOW), index_map=lambda i: (i, 0))],
            out_specs=[],
            core_axis_name=("core", "subcore"),
            dimension_semantics=(pltpu.PARALLEL,),
        )(src_hbm, idx_hbm)

    return scatter(src, idx.reshape(n // GATHER_WINDOW, GATHER_WINDOW))


def _slab_row_index(rows):
    sub = jnp.arange(SLAB[0], dtype=jnp.int32)
    return (rows[:, None] * SLAB[0] + sub[None, :]).reshape(-1)


def _scatter_slabs(slabs, rows, out_rows):
    out = _scatter_rows(slabs.reshape(-1, SLAB[1]), _slab_row_index(rows), out_rows * SLAB[0])
    return out.reshape((out_rows,) + SLAB)


def _gather_slabs(slabs, rows):
    out = _gather_rows(slabs.reshape(-1, SLAB[1]), _slab_row_index(rows))
    return out.reshape((rows.shape[0],) + SLAB)


def _gmm_kernel(te_ref, xs_ref, wg_ref, wu_ref, wd_ref, o_ref, acc_ref, xb_ref, *, n_tiles):
    i = pl.program_id(0)
    f = pl.program_id(1)

    @pl.when(f == 0)
    def _():
        acc_ref[...] = jnp.zeros_like(acc_ref)
        row = lax.broadcasted_iota(jnp.int32, (xb_ref.shape[0], 1), 0)
        live = row < te_ref[n_tiles + 1 + i]
        xb_ref[...] = jnp.where(live, _from_slabs(xs_ref), 0.0).astype(BF16)

    @pl.when(i < te_ref[n_tiles])
    def _():
        xb = xb_ref[...]
        hid = _silu(_dot(xb, wg_ref[...])) * _dot(xb, wu_ref[...])
        acc_ref[...] += _dot(hid.astype(BF16), wd_ref[...])

    @pl.when(f == pl.num_programs(1) - 1)
    def _():
        _to_slabs(acc_ref[...], o_ref)


def _grouped_swiglu(tile_expert, xs, wg, wu, wd, tm, tf):
    m = xs.shape[0]
    d_ff = wg.shape[2]
    n_tiles = m // tm
    grid_spec = pltpu.PrefetchScalarGridSpec(
        num_scalar_prefetch=1,
        grid=(n_tiles, d_ff // tf),
        in_specs=[pl.BlockSpec((tm,) + SLAB, lambda i, f, te: (i, 0, 0)),
                  pl.BlockSpec((None, D_MODEL, tf), lambda i, f, te: (te[i], 0, f)),
                  pl.BlockSpec((None, D_MODEL, tf), lambda i, f, te: (te[i], 0, f)),
                  pl.BlockSpec((None, tf, D_MODEL), lambda i, f, te: (te[i], f, 0))],
        out_specs=pl.BlockSpec((tm,) + SLAB, lambda i, f, te: (i, 0, 0)),
        scratch_shapes=[pltpu.VMEM((tm, D_MODEL), F32), pltpu.VMEM((tm, D_MODEL), BF16)])
    return pl.pallas_call(
        functools.partial(_gmm_kernel, n_tiles=n_tiles),
        grid_spec=grid_spec,
        out_shape=jax.ShapeDtypeStruct((m,) + SLAB, F32),
        compiler_params=_cparams("parallel", "arbitrary"),
        name="grouped_swiglu",
    )(tile_expert, xs, wg, wu, wd)


def _combine_kernel(x_ref, y0_ref, y1_ref, meta_ref, p_ref, pg_ref, pp_ref, lg_ref, lb_ref, *rest):
    o_ref = rest[-1]
    x = x_ref[...]
    meta = meta_ref[...]
    g0 = meta[:, META_G0:META_G0 + 1]
    g1 = meta[:, META_G1:META_G1 + 1]
    ple = (jax.nn.sigmoid(_dot(x.astype(BF16), pg_ref[...]))
           * _dot(p_ref[...].astype(BF16), pp_ref[...]))
    o = DEEPNORM_ALPHA * x + g0 * _from_slabs(y0_ref) + g1 * _from_slabs(y1_ref) + ple
    o_ref[...] = _layer_norm(o, lg_ref[...], lb_ref[...])


def _combine_ln2(x2, gathered, meta, p2, pg, pp, lg, lb, tm, first_block, out_prev):
    n = x2.shape[0]
    ple_dim = p2.shape[1]
    nb = gathered.shape[0] // 2 // tm

    def tok(width):
        return pl.BlockSpec((tm, width), lambda i: (i + first_block, 0))

    def const(shape):
        return pl.BlockSpec(shape, lambda i: (0, 0))

    operands = [x2, gathered, gathered, meta, p2, pg, pp, lg, lb]
    in_specs = [tok(D_MODEL), pl.BlockSpec((tm,) + SLAB, lambda i: (i, 0, 0)),
                pl.BlockSpec((tm,) + SLAB, lambda i: (i + nb, 0, 0)),
                tok(128), tok(ple_dim), const((D_MODEL, D_MODEL)), const((ple_dim, D_MODEL)),
                const((1, D_MODEL)), const((1, D_MODEL))]
    aliases = {}
    if out_prev is not None:
        operands.append(out_prev)
        in_specs.append(pl.BlockSpec(memory_space=pl.ANY))
        aliases = {9: 0}
    return pl.pallas_call(
        _combine_kernel,
        grid=(nb,),
        in_specs=in_specs,
        out_specs=tok(D_MODEL),
        out_shape=jax.ShapeDtypeStruct((n, D_MODEL), F32),
        input_output_aliases=aliases,
        compiler_params=_cparams("parallel"),
        name="combine_ple_ln2",
    )(*operands)


def _moe(x2, w_router, wg, wu, wd, p2, pg, pp, lg, lb, tm, tg, tf):
    n = x2.shape[0]
    wr = jnp.pad(w_router, ((0, 0), (0, 128 - N_EXPERTS)))
    meta, counts, x_slabs = _router(x2, wr, tm)
    cnt = counts[0, :N_EXPERTS].astype(jnp.int32)
    padded = ((cnt + tg - 1) // tg) * tg
    ends = jnp.cumsum(padded)
    starts = ends - padded
    experts = jnp.arange(N_EXPERTS, dtype=jnp.int32)

    def dest(e_lane, rank_lane):
        e = meta[:, e_lane].astype(jnp.int32)
        start = jnp.sum(jnp.where(e[:, None] == experts[None, :], starts[None, :], 0), axis=1)
        return start + meta[:, rank_lane].astype(jnp.int32)

    dest01 = jnp.concatenate([dest(META_E0, META_RANK0), dest(META_E1, META_RANK1)])
    m = 2 * n + N_EXPERTS * tg
    n_tiles = m // tg
    tile_start = jnp.arange(n_tiles, dtype=jnp.int32) * tg
    tile_expert = jnp.minimum(
        jnp.sum((ends[None, :] <= tile_start[:, None]).astype(jnp.int32), axis=1), N_EXPERTS - 1)
    onehot = tile_expert[:, None] == experts[None, :]
    tile_end = jnp.sum(jnp.where(onehot, (starts + cnt)[None, :], 0), axis=1)
    live_rows = jnp.clip(tile_end - tile_start, 0, tg)
    prefetch = jnp.concatenate([tile_expert, (ends[-1] // tg)[None], live_rows]).astype(jnp.int32)
    xs = _scatter_slabs(x_slabs, dest01, m)
    ys = _grouped_swiglu(prefetch, xs, wg, wu, wd, tg, tf)
    out = None
    half = n // 2
    for part in range(2):
        lo = part * half
        rows = jnp.concatenate([dest01[lo:lo + half], dest01[n + lo:n + lo + half]])
        out = _combine_ln2(x2, _gather_slabs(ys, rows), meta, p2, pg, pp, lg, lb, tm,
                           lo // tm, out)
    return out


def _ffn_kernel(x_ref, wg_ref, wu_ref, wd_ref, p_ref, pg_ref, pp_ref, lg_ref, lb_ref,
                o_ref, acc_ref, xb_ref):
    f = pl.program_id(1)

    @pl.when(f == 0)
    def _():
        acc_ref[...] = jnp.zeros_like(acc_ref)
        xb_ref[...] = x_ref[...].astype(BF16)

    xb = xb_ref[...]
    hid = _silu(_dot(xb, wg_ref[...])) * _dot(xb, wu_ref[...])
    acc_ref[...] += _dot(hid.astype(BF16), wd_ref[...])

    @pl.when(f == pl.num_programs(1) - 1)
    def _():
        ple = (jax.nn.sigmoid(_dot(xb, pg_ref[...]))
               * _dot(p_ref[...].astype(BF16), pp_ref[...]))
        o = DEEPNORM_ALPHA * x_ref[...] + acc_ref[...] + ple
        o_ref[...] = _layer_norm(o, lg_ref[...], lb_ref[...])


def _ffn(x2, wg, wu, wd, p2, pg, pp, lg, lb, tm, tf):
    n = x2.shape[0]
    d_ff = wg.shape[1]
    ple_dim = p2.shape[1]

    def tok(width):
        return pl.BlockSpec((tm, width), lambda i, f: (i, 0))

    def const(shape):
        return pl.BlockSpec(shape, lambda i, f: (0, 0))

    return pl.pallas_call(
        _ffn_kernel,
        grid=(n // tm, d_ff // tf),
        in_specs=[tok(D_MODEL),
                  pl.BlockSpec((D_MODEL, tf), lambda i, f: (0, f)),
                  pl.BlockSpec((D_MODEL, tf), lambda i, f: (0, f)),
                  pl.BlockSpec((tf, D_MODEL), lambda i, f: (f, 0)),
                  tok(ple_dim), const((D_MODEL, D_MODEL)), const((ple_dim, D_MODEL)),
                  const((1, D_MODEL)), const((1, D_MODEL))],
        out_specs=tok(D_MODEL),
        out_shape=jax.ShapeDtypeStruct((n, D_MODEL), F32),
        scratch_shapes=[pltpu.VMEM((tm, D_MODEL), F32), pltpu.VMEM((tm, D_MODEL), BF16)],
        compiler_params=_cparams("parallel", "arbitrary"),
        name="ffn_ple_ln2",
    )(x2, wg, wu, wd, p2, pg, pp, lg, lb)


def _prep_w_in(w):
    a, b, c, d, g = jnp.split(w, [768, 1280, 2048, 3336], axis=1)
    z, xs, bc, dt = jnp.split(d, [512, 1024, 1280], axis=1)

    def scale_q(t, scale):
        return jnp.concatenate([t[:, :MIX_W] * scale, t[:, MIX_W:]], axis=1)

    qk_scale = HEAD_DIM ** -0.5
    pad = jnp.zeros((w.shape[0], IN_COLS - (COL_DT + 8)), w.dtype)
    mixers = jnp.concatenate([scale_q(a, qk_scale * LOG2_E), b, scale_q(c, qk_scale), z, xs, bc, dt,
                              pad], axis=1)
    return mixers.astype(BF16), g.astype(BF16)


def _row(v):
    return v.reshape(1, -1).astype(F32)


def _per_head(v):
    return _row(jnp.repeat(v, HEAD_DIM))


def kernel(x, p, w_in, w_br_a, w_br_b, w_br_c, w_br_d, w_out, sg_ln_g, sg_ln_b, sg_w, sg_b,
           ca_rel_bias, ssd_conv_w, ssd_conv_b, ssd_dt_bias, ssd_a_log, ssd_d, ssd_norm_g,
           ln1_g, ln1_b, ffn_w_gate, ffn_w_up, ffn_w_down, moe_router, moe_w_gate, moe_w_up,
           moe_w_down, ple_w_gate, ple_w_proj, ln2_g, ln2_b):
    bsz, seq, _ = x.shape
    n = bsz * seq
    x2 = x.reshape(n, D_MODEL)
    tm_proj = min(1024, n)
    tm = min(512, n)
    sb_blk = min(256, seq)
    ssd_q = min(256, seq)
    sg_tb = min(1024, seq)

    for i in range(DEPTH):
        w_mix, w_gate = _prep_w_in(w_in[i])
        h = _inproj(x2, w_mix, tm_proj, IN_COLS)
        ya = _stick_breaking(h, bsz, seq, sb_blk)
        sg_bias = jnp.repeat(jnp.transpose(sg_b[i]), HEAD_DIM, axis=1).astype(F32)
        yb = _spatial_gating(h, _row(sg_ln_g[i]), _row(sg_ln_b[i]), sg_w[i], sg_bias, sg_tb)
        yc = _band_attention(h, _band_bias(ca_rel_bias[i]), seq)
        cw, cb = ssd_conv_w[i], ssd_conv_b[i]
        yd = _ssd(h, cw[:, :SSD_INNER], _row(cb[:SSD_INNER]), cw[:, SSD_INNER:],
                  _row(cb[SSD_INNER:]), _per_head(ssd_dt_bias[i]), _per_head(ssd_a_log[i]),
                  _per_head(ssd_d[i]), _row(ssd_norm_g[i]), bsz, seq, ssd_q)
        x2 = _merge(x2, w_gate, ya, yb, yc, yd, w_br_a[i].astype(BF16), w_br_b[i].astype(BF16),
                    w_br_c[i].astype(BF16), w_br_d[i].astype(BF16), w_out[i].astype(BF16),
                    _row(ln1_g[i]), _row(ln1_b[i]), tm_proj)
        p2 = p[i].reshape(n, -1)
        pg, pp = ple_w_gate[i].astype(BF16), ple_w_proj[i].astype(BF16)
        j = i // 2
        if i % 2 == 0:
            x2 = _ffn(x2, ffn_w_gate[j].astype(BF16), ffn_w_up[j].astype(BF16),
                      ffn_w_down[j].astype(BF16), p2, pg, pp, _row(ln2_g[i]), _row(ln2_b[i]),
                      tm, 2816)
        else:
            x2 = _moe(x2, moe_router[j], moe_w_gate[j].astype(BF16), moe_w_up[j].astype(BF16),
                      moe_w_down[j].astype(BF16), p2, pg, pp, _row(ln2_g[i]), _row(ln2_b[i]),
                      tm, min(1024, n), 1792)
    return x2.reshape(bsz, seq, D_MODEL)
```

```python
import functools

import jax
import jax.numpy as jnp
from jax import lax
from jax.experimental import pallas as pl
from jax.experimental.pallas import tpu as pltpu
from jax.experimental.pallas import tpu_sc as plsc

F32 = jnp.float32
BF16 = jnp.bfloat16

D_MODEL = 1024
DEPTH = 2
LN_EPS = 1e-5
DEEPNORM_ALPHA = (2 * DEPTH) ** 0.25
HEAD_DIM = 64
MIX_W = 256
SG_CHUNK = 128
BAND_CHUNK = 64
BAND_LEFT = 8
BAND_REL_MAX = 256
SSD_INNER = 512
SSD_HEADS = 8
SSD_BC = 256
N_EXPERTS = 8
NEG_BIG = -1e30
LOG2_E = 1.4426950408889634
EXP2_UNDERFLOW = -160.0

COL_A = 0
COL_B = 768
COL_C = 1280
COL_Z = 2048
COL_XS = 2560
COL_BC = 3072
COL_DT = 3328
IN_COLS = 3584

VMEM_LIMIT = 56 * 1024 * 1024


def _cparams(*sem):
    return pltpu.CompilerParams(dimension_semantics=sem, vmem_limit_bytes=VMEM_LIMIT)


def _nt_dot(a, b):
    return lax.dot_general(a, b, (((1,), (1,)), ((), ())), preferred_element_type=F32)


def _dot(a, b):
    return jnp.dot(a, b, preferred_element_type=F32)


def _softplus(x):
    return jnp.maximum(x, 0.0) + jnp.log(1.0 + jnp.exp(-jnp.abs(x)))


def _softplus2(x):
    return jnp.maximum(x, 0.0) + jnp.log2(1.0 + jnp.exp2(-jnp.abs(x)))


def _silu(x):
    return x * jax.nn.sigmoid(x)


def _split2(x):
    hi = x.astype(BF16)
    lo = (x - hi.astype(F32)).astype(BF16)
    return hi, lo


def _split3(x):
    hi = x.astype(BF16)
    r = x - hi.astype(F32)
    mid = r.astype(BF16)
    lo = (r - mid.astype(F32)).astype(BF16)
    return hi, mid, lo


def _layer_norm(x, g, b):
    mu = jnp.mean(x, axis=-1, keepdims=True)
    xc = x - mu
    var = jnp.mean(xc * xc, axis=-1, keepdims=True)
    return xc * lax.rsqrt(var + LN_EPS) * g + b


def _inproj_kernel(x_ref, w_ref, o_ref):
    o_ref[...] = _dot(x_ref[...].astype(BF16), w_ref[...]).astype(o_ref.dtype)


def _inproj(x2, w, tm, tn):
    n, d = x2.shape
    nc = w.shape[1]
    return pl.pallas_call(
        _inproj_kernel,
        grid=(n // tm, nc // tn),
        in_specs=[pl.BlockSpec((tm, d), lambda i, j: (i, 0)),
                  pl.BlockSpec((d, tn), lambda i, j: (0, j))],
        out_specs=pl.BlockSpec((tm, tn), lambda i, j: (i, j)),
        out_shape=jax.ShapeDtypeStruct((n, nc), BF16),
        compiler_params=_cparams("parallel", "arbitrary"),
        name="inproj",
    )(x2, w)


def _sb_kernel(q_ref, k_ref, v_ref, o_ref, acc_ref, run_ref, kmax_ref, *, blk, seq):
    qi = pl.program_id(1)
    lane = lax.broadcasted_iota(jnp.int32, (1, MIX_W), 1)
    head_mask = [(lane // HEAD_DIM) == h for h in range(4)]

    def head_sq_norms(t):
        tf = t.astype(F32)
        sq = tf * tf
        return [jnp.sum(jnp.where(head_mask[h], sq, 0.0), axis=-1, keepdims=True)
                for h in range(4)]

    @pl.when(qi == 0)
    def _():
        kmax_ref[...] = jnp.zeros_like(kmax_ref)

        def scan(c, carry):
            norms = head_sq_norms(k_ref[pl.ds(pl.multiple_of(c * blk, blk), blk), :])
            for h in range(4):
                kmax_ref[h] = jnp.maximum(kmax_ref[h], jnp.max(norms[h], axis=0, keepdims=True))
            return carry

        lax.fori_loop(0, seq // blk, scan, 0)

    q = q_ref[...]
    q_heads = [jnp.where(head_mask[h], q, jnp.zeros_like(q)) for h in range(4)]
    q_norms = head_sq_norms(q)
    z_bound = [jnp.sqrt(q_norms[h] * kmax_ref[h]) * 1.001 + 1e-3 for h in range(4)]
    row = lax.broadcasted_iota(jnp.int32, (blk, blk), 0)
    col = lax.broadcasted_iota(jnp.int32, (blk, blk), 1)
    below_diag = col < row
    suffix = jnp.where(row >= col, 1.0, 0.0).astype(BF16)
    suffix2 = jnp.concatenate([suffix, suffix], axis=0)

    acc_ref[...] = jnp.zeros_like(acc_ref)
    run_ref[...] = jnp.zeros_like(run_ref)

    n_parts = 4
    part_rows = blk // n_parts

    def process(j, masked, n_live):
        rows = slice(0, n_live * part_rows)
        kb = qi - j
        start = pl.multiple_of(kb * blk, blk)
        k_blk = k_ref[pl.ds(start, blk), :]
        v_blk = v_ref[pl.ds(start, blk), :]
        weights = []
        slack = [None] * n_live
        for h in range(4):
            z = _nt_dot(q_heads[h][rows], k_blk)
            sp = _softplus2(z)
            if masked:
                sp = jnp.where(below_diag[rows], sp, 0.0)
            cs = _dot(jnp.concatenate(_split2(sp), axis=1), suffix2)
            run = run_ref[h, rows, :]
            arg = z - cs - run
            if masked:
                arg = jnp.where(below_diag[rows], arg, NEG_BIG)
            w = jnp.exp2(arg)
            run = run + cs[:, 0:1]
            run_ref[h, rows, :] = run
            weights.append(w.astype(BF16))
            gap = z_bound[h][rows] - run
            for p in range(n_live):
                part_slack = jnp.max(gap[p * part_rows:(p + 1) * part_rows])
                slack[p] = part_slack if slack[p] is None else jnp.maximum(slack[p], part_slack)
        wcat = jnp.concatenate(weights, axis=1)
        vcat = jnp.concatenate(
            [jnp.where(head_mask[h], v_blk, jnp.zeros_like(v_blk)) for h in range(4)], axis=0)
        acc_ref[rows, :] += _dot(wcat, vcat)
        live = jnp.int32(0)
        for p in range(n_live):
            live = jnp.where(slack[p] > EXP2_UNDERFLOW, jnp.int32(p + 1), live)
        return live

    def cond(carry):
        j, n_live = carry
        return jnp.logical_and(j <= qi, n_live > 0)

    def body(carry):
        j, n_live = carry
        step = functools.partial(process, j, False)
        n_live = lax.cond(
            n_live == 4, lambda: step(4),
            lambda: lax.cond(n_live == 3, lambda: step(3),
                             lambda: lax.cond(n_live == 2, lambda: step(2), lambda: step(1))))
        return j + 1, n_live

    lax.while_loop(cond, body, (jnp.int32(1), process(0, True, n_parts)))
    o_ref[...] = acc_ref[...].astype(o_ref.dtype)


def _stick_breaking(h, bsz, seq, blk):
    n = bsz * seq
    nq = seq // blk
    cq, ck, cv = (COL_A // MIX_W + i for i in range(3))
    return pl.pallas_call(
        functools.partial(_sb_kernel, blk=blk, seq=seq),
        grid=(bsz, nq),
        in_specs=[pl.BlockSpec((blk, MIX_W), lambda b, i: (b * nq + i, cq)),
                  pl.BlockSpec((seq, MIX_W), lambda b, i: (b, ck)),
                  pl.BlockSpec((seq, MIX_W), lambda b, i: (b, cv))],
        out_specs=pl.BlockSpec((blk, MIX_W), lambda b, i: (b * nq + i, 0)),
        out_shape=jax.ShapeDtypeStruct((n, MIX_W), BF16),
        scratch_shapes=[pltpu.VMEM((blk, MIX_W), F32), pltpu.VMEM((4, blk, 1), F32),
                        pltpu.VMEM((4, 1, 1), F32)],
        compiler_params=_cparams("parallel", "arbitrary"),
        name="stick_breaking",
    )(h, h, h)


def _gelu_tanh(x):
    return 0.5 * x * (1.0 + jnp.tanh(0.7978845608028654 * (x + 0.044715 * (x * x * x))))


def _sg_kernel(u_ref, v_ref, g_ref, b_ref, w_ref, bias_ref, o_ref, *, tb):
    u = _gelu_tanh(u_ref[...].astype(F32))
    v = _gelu_tanh(v_ref[...].astype(F32))
    vn = _layer_norm(v, g_ref[...], b_ref[...]).astype(BF16)
    lane = lax.broadcasted_iota(jnp.int32, (1, MIX_W), 1)
    group_mask = [(lane // HEAD_DIM) == g for g in range(4)]
    row = lax.broadcasted_iota(jnp.int32, (SG_CHUNK, SG_CHUNK), 0)
    col = lax.broadcasted_iota(jnp.int32, (SG_CHUNK, SG_CHUNK), 1)
    causal = col <= row
    wcat = jnp.concatenate(
        [jnp.where(causal, w_ref[g], 0.0).astype(BF16) for g in range(4)], axis=1)
    bias = bias_ref[...]
    for c in range(tb // SG_CHUNK):
        sl = slice(c * SG_CHUNK, (c + 1) * SG_CHUNK)
        vc = vn[sl, :]
        vstack = jnp.concatenate(
            [jnp.where(group_mask[g], vc, jnp.zeros_like(vc)) for g in range(4)], axis=0)
        mixed = _dot(wcat, vstack) + bias
        o_ref[sl, :] = (u[sl, :] * mixed).astype(o_ref.dtype)


def _spatial_gating(h, ln_g, ln_b, w_s, bias_full, tb):
    n = h.shape[0]
    cu, cv = COL_B // MIX_W, COL_B // MIX_W + 1
    return pl.pallas_call(
        functools.partial(_sg_kernel, tb=tb),
        grid=(n // tb,),
        in_specs=[pl.BlockSpec((tb, MIX_W), lambda i: (i, cu)),
                  pl.BlockSpec((tb, MIX_W), lambda i: (i, cv)),
                  pl.BlockSpec((1, MIX_W), lambda i: (0, 0)),
                  pl.BlockSpec((1, MIX_W), lambda i: (0, 0)),
                  pl.BlockSpec((4, SG_CHUNK, SG_CHUNK), lambda i: (0, 0, 0)),
                  pl.BlockSpec((SG_CHUNK, MIX_W), lambda i: (0, 0))],
        out_specs=pl.BlockSpec((tb, MIX_W), lambda i: (i, 0)),
        out_shape=jax.ShapeDtypeStruct((n, MIX_W), BF16),
        compiler_params=_cparams("parallel"),
        name="spatial_gating",
    )(h, h, ln_g, ln_b, w_s, bias_full)


BAND_TQ = 256
BAND_PREV = BAND_LEFT * BAND_CHUNK
BAND_WIN = BAND_PREV + BAND_TQ


def _band_kernel(q_ref, k2_ref, k1_ref, k0_ref, v2_ref, v1_ref, v0_ref, bias_ref, o_ref, *, nblk):
    bi = pl.program_id(0) % nblk
    lane = lax.broadcasted_iota(jnp.int32, (1, MIX_W), 1)
    q = q_ref[...]
    kcat = jnp.concatenate([k2_ref[...], k1_ref[...], k0_ref[...]], axis=0)
    vcat = jnp.concatenate([v2_ref[...], v1_ref[...], v0_ref[...]], axis=0)
    col = lax.broadcasted_iota(jnp.int32, (1, BAND_WIN), 1)
    in_seq = col >= (2 - jnp.minimum(bi, 2)) * BAND_TQ
    out = jnp.zeros((BAND_TQ, MIX_W), F32)
    for h in range(4):
        hm = (lane // HEAD_DIM) == h
        s = _nt_dot(jnp.where(hm, q, jnp.zeros_like(q)), kcat) + bias_ref[h]
        s = jnp.where(in_seq, s, NEG_BIG)
        m = jnp.max(s, axis=-1, keepdims=True)
        p = jnp.exp(s - m)
        l = jnp.sum(p, axis=-1, keepdims=True)
        o = _dot(p.astype(BF16), vcat) / l
        out = jnp.where(hm, o, out)
    o_ref[...] = out.astype(o_ref.dtype)


def _band_attention(h, bias_full, seq):
    n = h.shape[0]
    nblk = seq // BAND_TQ
    cq, ck, cv = (COL_C // MIX_W + i for i in range(3))

    def prev(i, d):
        return i - jnp.minimum(i % nblk, d)

    def spec(c, d):
        return pl.BlockSpec((BAND_TQ, MIX_W), lambda i: (prev(i, d), c))

    return pl.pallas_call(
        functools.partial(_band_kernel, nblk=nblk),
        grid=(n // BAND_TQ,),
        in_specs=[spec(cq, 0), spec(ck, 2), spec(ck, 1), spec(ck, 0),
                  spec(cv, 2), spec(cv, 1), spec(cv, 0),
                  pl.BlockSpec((4, BAND_TQ, BAND_WIN), lambda i: (0, 0, 0))],
        out_specs=pl.BlockSpec((BAND_TQ, MIX_W), lambda i: (i, 0)),
        out_shape=jax.ShapeDtypeStruct((n, MIX_W), BF16),
        compiler_params=_cparams("parallel"),
        name="band_attention",
    )(h, h, h, h, h, h, h, bias_full)


def _band_bias(rel_bias):
    period = 1024
    u = jnp.arange(period)
    rel = jnp.where(u <= BAND_WIN, BAND_PREV - u, BAND_PREV + period - u)
    row0 = rel_bias.astype(F32)[:, jnp.clip(rel, -(BAND_CHUNK - 1), BAND_REL_MAX) + (BAND_CHUNK - 1)]
    skew = jnp.tile(row0, (1, BAND_TQ))[:, :BAND_TQ * (period - 1)]
    bias = skew.reshape(4, BAND_TQ, period - 1)[:, :, :BAND_WIN]
    tc = (jnp.arange(BAND_TQ)[:, None] + BAND_PREV) // BAND_CHUNK
    sc = jnp.arange(BAND_WIN)[None, :] // BAND_CHUNK
    in_band = jnp.logical_and(sc <= tc, sc >= tc - BAND_LEFT)
    return jnp.where(in_band[None], bias, NEG_BIG)


def _ssd_kernel(z_ref, xs_ref, bc_ref, dt_ref, cwx_ref, cbx_ref, cwb_ref, cbb_ref, dtb_ref,
                alog_ref, dsk_ref, ng_ref, o_ref, xpad_ref, bpad_ref, state_ref, expand_ref,
                pick_ref, tri_ref, eye_ref, *, q):
    c = pl.program_id(1)
    row = lax.broadcasted_iota(jnp.int32, (q, q), 0)
    col = lax.broadcasted_iota(jnp.int32, (q, q), 1)
    causal = col <= row

    @pl.when(c == 0)
    def _():
        xpad_ref[0:8, :] = jnp.zeros((8, SSD_INNER), F32)
        bpad_ref[0:8, :] = jnp.zeros((8, SSD_BC), F32)
        state_ref[...] = jnp.zeros_like(state_ref)
        r128 = lax.broadcasted_iota(jnp.int32, (128, SSD_INNER), 0)
        c512 = lax.broadcasted_iota(jnp.int32, (128, SSD_INNER), 1)
        expand_ref[...] = jnp.where(c512 // HEAD_DIM == r128, 1.0, 0.0).astype(BF16)
        pick_ref[...] = jnp.where(c512 == r128 * HEAD_DIM, 1.0, 0.0).astype(BF16)
        tri_ref[...] = jnp.where(causal, 1.0, 0.0).astype(BF16)
        eye_ref[...] = jnp.where(lax.broadcasted_iota(jnp.int32, (128, 128), 0)
                                 == lax.broadcasted_iota(jnp.int32, (128, 128), 1),
                                 1.0, 0.0).astype(BF16)

    xpad_ref[8:8 + q, :] = xs_ref[...].astype(F32)
    bpad_ref[8:8 + q, :] = bc_ref[...].astype(F32)

    def conv_silu(pad_ref, w_ref, b_ref):
        xp = pad_ref[...]
        acc = b_ref[...] + w_ref[3:4, :] * xp[8:8 + q, :]
        for d in (1, 2, 3):
            acc = acc + w_ref[3 - d:4 - d, :] * pltpu.roll(xp, d, axis=0)[8:8 + q, :]
        return _silu(acc)

    xs = conv_silu(xpad_ref, cwx_ref, cbx_ref)
    bc = conv_silu(bpad_ref, cwb_ref, cbb_ref)
    xpad_ref[0:8, :] = xpad_ref[q:q + 8, :]
    bpad_ref[0:8, :] = bpad_ref[q:q + 8, :]
    bm = bc[:, 0:128].astype(BF16)
    cm = bc[:, 128:256].astype(BF16)

    pick = pick_ref[...]
    tri = tri_ref[...]
    dt = _softplus(_dot(dt_ref[...], expand_ref[...]) + dtb_ref[...])
    da = dt * (-jnp.exp(alog_ref[...]))
    da_hi, da_lo = _split2(da)
    acs = _dot(tri, da_hi) + _dot(tri, da_lo)
    a_hi, a_mid, a_lo = _split3(acs)
    acs_t = _nt_dot(pick, a_hi) + _nt_dot(pick, a_mid) + _nt_dot(pick, a_lo)
    xdt = xs * dt

    bm_t = _nt_dot(eye_ref[...], bm).astype(BF16)
    lane128 = lax.broadcasted_iota(jnp.int32, (1, 128), 1)
    lane256 = lax.broadcasted_iota(jnp.int32, (1, 256), 1)

    y_groups = []
    for g in range(2):
        gm = (lane128 // HEAD_DIM) == g
        cb = _nt_dot(jnp.where(gm, cm, jnp.zeros_like(cm)), bm)
        xg = xdt[:, g * 256:(g + 1) * 256].astype(BF16)
        ms, xstack = [], []
        for hh in range(4):
            head = g * 4 + hh
            seg = acs[:, head * HEAD_DIM:head * HEAD_DIM + 1] - acs_t[head:head + 1, :]
            decay = jnp.exp(jnp.where(causal, seg, NEG_BIG))
            ms.append((cb * decay).astype(BF16))
            xstack.append(jnp.where((lane256 // HEAD_DIM) == hh, xg, jnp.zeros_like(xg)))
        y_groups.append(_dot(jnp.concatenate(ms, axis=1), jnp.concatenate(xstack, axis=0)))
    y_diag = jnp.concatenate(y_groups, axis=1)

    state = state_ref[...]
    y_off = _dot(cm, state.astype(BF16)) * jnp.exp(acs)
    acs_last = acs[q - 1:q, :]
    xw = (xdt * jnp.exp(acs_last - acs)).astype(BF16)
    keep = (lax.broadcasted_iota(jnp.int32, (128, SSD_INNER), 0) // HEAD_DIM
            == lax.broadcasted_iota(jnp.int32, (128, SSD_INNER), 1) // 256)
    state_ref[...] = jnp.where(keep, state * jnp.exp(acs_last) + _dot(bm_t, xw), 0.0)

    y = y_diag + y_off + xs * dsk_ref[...]
    y = y * _silu(z_ref[...].astype(F32))
    outs = []
    for g in range(2):
        yg = y[:, g * 256:(g + 1) * 256]
        outs.append(yg * lax.rsqrt(jnp.mean(yg * yg, axis=-1, keepdims=True) + LN_EPS))
    o_ref[...] = (jnp.concatenate(outs, axis=1) * ng_ref[...]).astype(o_ref.dtype)


def _ssd(h, conv_wx, conv_bx, conv_wb, conv_bb, dt_bias_e, a_log_e, d_e, norm_g, bsz, seq, q):
    n = bsz * seq
    nc = seq // q

    def tok(width, colblk):
        return pl.BlockSpec((q, width), lambda b, c: (b * nc + c, colblk))

    def const(shape):
        return pl.BlockSpec(shape, lambda b, c: (0, 0))

    return pl.pallas_call(
        functools.partial(_ssd_kernel, q=q),
        grid=(bsz, nc),
        in_specs=[tok(SSD_INNER, COL_Z // SSD_INNER), tok(SSD_INNER, COL_XS // SSD_INNER),
                  tok(SSD_BC, COL_BC // SSD_BC), tok(128, COL_DT // 128),
                  const((4, SSD_INNER)), const((1, SSD_INNER)),
                  const((4, SSD_BC)), const((1, SSD_BC)),
                  const((1, SSD_INNER)), const((1, SSD_INNER)), const((1, SSD_INNER)),
                  const((1, SSD_INNER))],
        out_specs=pl.BlockSpec((q, SSD_INNER), lambda b, c: (b * nc + c, 0)),
        out_shape=jax.ShapeDtypeStruct((n, SSD_INNER), BF16),
        scratch_shapes=[pltpu.VMEM((q + 8, SSD_INNER), F32), pltpu.VMEM((q + 8, SSD_BC), F32),
                        pltpu.VMEM((128, SSD_INNER), F32), pltpu.VMEM((128, SSD_INNER), BF16),
                        pltpu.VMEM((128, SSD_INNER), BF16), pltpu.VMEM((q, q), BF16),
                        pltpu.VMEM((128, 128), BF16)],
        compiler_params=_cparams("parallel", "arbitrary"),
        name="ssd",
    )(h, h, h, h, conv_wx, conv_bx, conv_wb, conv_bb, dt_bias_e, a_log_e, d_e, norm_g)


def _merge_kernel(x_ref, wg_ref, ya_ref, yb_ref, yc_ref, yd_ref, wa_ref, wb_ref, wc_ref, wd_ref,
                  wo_ref, lg_ref, lb_ref, o_ref):
    x = x_ref[...]
    xb = x.astype(BF16)
    merged = None
    for i, (y_ref, w_ref) in enumerate(((ya_ref, wa_ref), (yb_ref, wb_ref), (yc_ref, wc_ref),
                                        (yd_ref, wd_ref))):
        gate = jax.nn.sigmoid(_dot(xb, wg_ref[:, i * D_MODEL:(i + 1) * D_MODEL]))
        term = gate * _dot(y_ref[...], w_ref[...])
        merged = term if merged is None else merged + term
    o = DEEPNORM_ALPHA * x + _dot(merged.astype(BF16), wo_ref[...])
    o_ref[...] = _layer_norm(o, lg_ref[...], lb_ref[...])


def _merge(x2, w_gate, ya, yb, yc, yd, wa, wb, wc, wd, wo, lg, lb, tm):
    n = x2.shape[0]

    def tok(width):
        return pl.BlockSpec((tm, width), lambda i: (i, 0))

    def const(shape):
        return pl.BlockSpec(shape, lambda i: (0, 0))

    return pl.pallas_call(
        _merge_kernel,
        grid=(n // tm,),
        in_specs=[tok(D_MODEL), const((D_MODEL, 4 * D_MODEL)), tok(MIX_W), tok(MIX_W), tok(MIX_W),
                  tok(SSD_INNER), const((MIX_W, D_MODEL)), const((MIX_W, D_MODEL)),
                  const((MIX_W, D_MODEL)), const((SSD_INNER, D_MODEL)),
                  const((D_MODEL, D_MODEL)), const((1, D_MODEL)), const((1, D_MODEL))],
        out_specs=tok(D_MODEL),
        out_shape=jax.ShapeDtypeStruct((n, D_MODEL), F32),
        compiler_params=_cparams("parallel"),
        name="merge_ln1",
    )(x2, w_gate, ya, yb, yc, yd, wa, wb, wc, wd, wo, lg, lb)


META_E0, META_E1, META_RANK0, META_RANK1, META_G0, META_G1 = range(6)

SLAB = (8, 128)


def _to_slabs(x, slab_ref):
    slab_ref[...] = x.reshape((x.shape[0],) + SLAB)


def _from_slabs(slab_ref):
    return slab_ref[...].reshape(slab_ref.shape[0], SLAB[0] * SLAB[1])


def _router_kernel(x_ref, w_ref, meta_ref, cnt_ref, slab_ref, base_ref, *, tm):
    @pl.when(pl.program_id(0) == 0)
    def _():
        base_ref[...] = jnp.zeros_like(base_ref)

    x = x_ref[...]
    _to_slabs(x, slab_ref)
    logits = jnp.dot(x, w_ref[...], preferred_element_type=F32,
                     precision=lax.Precision.HIGHEST)
    lane = lax.broadcasted_iota(jnp.int32, logits.shape, 1)
    lg = jnp.where(lane < N_EXPERTS, logits, -jnp.inf)
    m1 = jnp.max(lg, axis=-1, keepdims=True)
    i1 = jnp.min(jnp.where(lg == m1, lane, 128), axis=-1, keepdims=True)
    first = lane == i1
    lg2 = jnp.where(first, -jnp.inf, lg)
    m2 = jnp.max(lg2, axis=-1, keepdims=True)
    i2 = jnp.min(jnp.where(lg2 == m2, lane, 128), axis=-1, keepdims=True)
    second = lane == i2
    e2 = jnp.exp(m2 - m1)
    denom = 1.0 + e2
    sel = jnp.where(jnp.logical_or(first, second), 1.0, 0.0)
    row = lax.broadcasted_iota(jnp.int32, (tm, tm), 0)
    col = lax.broadcasted_iota(jnp.int32, (tm, tm), 1)
    strict_lower = jnp.where(col < row, 1.0, 0.0).astype(BF16)
    pos = base_ref[...] + _dot(strict_lower, sel.astype(BF16))
    rank0 = jnp.sum(jnp.where(first, pos, 0.0), axis=-1, keepdims=True)
    rank1 = jnp.sum(jnp.where(second, pos, 0.0), axis=-1, keepdims=True)
    base_ref[...] += jnp.sum(sel, axis=0, keepdims=True)
    cnt_ref[...] = jnp.broadcast_to(base_ref[...], cnt_ref.shape)
    fields = (i1.astype(F32), i2.astype(F32), rank0, rank1, 1.0 / denom, e2 / denom)
    meta = jnp.zeros(logits.shape, F32)
    for k, val in enumerate(fields):
        meta = jnp.where(lane == k, val, meta)
    meta_ref[...] = meta


def _router(x2, w_router_padded, tm):
    n = x2.shape[0]
    return pl.pallas_call(
        functools.partial(_router_kernel, tm=tm),
        grid=(n // tm,),
        in_specs=[pl.BlockSpec((tm, D_MODEL), lambda i: (i, 0)),
                  pl.BlockSpec((D_MODEL, 128), lambda i: (0, 0))],
        out_specs=[pl.BlockSpec((tm, 128), lambda i: (i, 0)),
                   pl.BlockSpec((8, 128), lambda i: (0, 0)),
                   pl.BlockSpec((tm,) + SLAB, lambda i: (i, 0, 0))],
        out_shape=[jax.ShapeDtypeStruct((n, 128), F32), jax.ShapeDtypeStruct((8, 128), F32),
                   jax.ShapeDtypeStruct((n,) + SLAB, F32)],
        scratch_shapes=[pltpu.VMEM((1, 128), F32)],
        compiler_params=_cparams("arbitrary"),
        name="router",
    )(x2, w_router_padded)


GATHER_WINDOW = 128


def _gather_rows(table, idx):
    n = idx.shape[0]
    mesh = plsc.VectorSubcoreMesh(core_axis_name="core", subcore_axis_name="subcore")

    @pl.kernel(out_type=jax.ShapeDtypeStruct((n, table.shape[1]), table.dtype), mesh=mesh)
    def gather(table_hbm, idx_hbm, out_hbm):
        def body(idx_vmem, out_vmem):
            pltpu.sync_copy(table_hbm.at[idx_vmem.at[0]], out_vmem)

        pltpu.emit_pipeline(
            body,
            grid=(n // GATHER_WINDOW,),
            in_specs=[pl.BlockSpec((1, GATHER_WINDOW), index_map=lambda i: (i, 0))],
            out_specs=[pl.BlockSpec((GATHER_WINDOW, table.shape[1]), index_map=lambda i: (i, 0))],
            core_axis_name=("core", "subcore"),
            dimension_semantics=(pltpu.PARALLEL,),
        )(idx_hbm, out_hbm)

    return gather(table, idx.reshape(n // GATHER_WINDOW, GATHER_WINDOW))


def _scatter_rows(src, idx, out_rows):
    n = idx.shape[0]
    src_blocks = src.shape[0] // GATHER_WINDOW
    mesh = plsc.VectorSubcoreMesh(core_axis_name="core", subcore_axis_name="subcore")

    @pl.kernel(out_type=jax.ShapeDtypeStruct((out_rows, src.shape[1]), src.dtype), mesh=mesh,
               scratch_types=[])
    def scatter(src_hbm, idx_hbm, out_hbm):
        def body(src_vmem, idx_vmem):
            pltpu.sync_copy(src_vmem, out_hbm.at[idx_vmem.at[0]])

        pltpu.emit_pipeline(
            body,
            grid=(n // GATHER_WINDOW,),
            in_specs=[pl.BlockSpec((GATHER_WINDOW, src.shape[1]),
                                   index_map=lambda i: (i % src_blocks, 0)),
                      pl.BlockSpec((1, GATHER_WINDOW), index_map=lambda i: (i, 0))],
            out_specs=[],
            core_axis_name=("core", "subcore"),
            dimension_semantics=(pltpu.PARALLEL,),
        )(src_hbm, idx_hbm)

    return scatter(src, idx.reshape(n // GATHER_WINDOW, GATHER_WINDOW))


def _slab_row_index(rows):
    sub = jnp.arange(SLAB[0], dtype=jnp.int32)
    return (rows[:, None] * SLAB[0] + sub[None, :]).reshape(-1)


def _scatter_slabs(slabs, rows, out_rows):
    out = _scatter_rows(slabs.reshape(-1, SLAB[1]), _slab_row_index(rows), out_rows * SLAB[0])
    return out.reshape((out_rows,) + SLAB)


def _gather_slabs(slabs, rows):
    out = _gather_rows(slabs.reshape(-1, SLAB[1]), _slab_row_index(rows))
    return out.reshape((rows.shape[0],) + SLAB)


def _gmm_kernel(te_ref, xs_ref, wg_ref, wu_ref, wd_ref, o_ref, acc_ref, xb_ref, *, n_tiles):
    i = pl.program_id(0)
    f = pl.program_id(1)

    @pl.when(f == 0)
    def _():
        acc_ref[...] = jnp.zeros_like(acc_ref)
        row = lax.broadcasted_iota(jnp.int32, (xb_ref.shape[0], 1), 0)
        live = row < te_ref[n_tiles + 1 + i]
        xb_ref[...] = jnp.where(live, _from_slabs(xs_ref), 0.0).astype(BF16)

    @pl.when(i < te_ref[n_tiles])
    def _():
        xb = xb_ref[...]
        hid = _silu(_dot(xb, wg_ref[...])) * _dot(xb, wu_ref[...])
        acc_ref[...] += _dot(hid.astype(BF16), wd_ref[...])

    @pl.when(f == pl.num_programs(1) - 1)
    def _():
        _to_slabs(acc_ref[...], o_ref)


def _grouped_swiglu(tile_expert, xs, wg, wu, wd, tm, tf):
    m = xs.shape[0]
    d_ff = wg.shape[2]
    n_tiles = m // tm
    grid_spec = pltpu.PrefetchScalarGridSpec(
        num_scalar_prefetch=1,
        grid=(n_tiles, d_ff // tf),
        in_specs=[pl.BlockSpec((tm,) + SLAB, lambda i, f, te: (i, 0, 0)),
                  pl.BlockSpec((None, D_MODEL, tf), lambda i, f, te: (te[i], 0, f)),
                  pl.BlockSpec((None, D_MODEL, tf), lambda i, f, te: (te[i], 0, f)),
                  pl.BlockSpec((None, tf, D_MODEL), lambda i, f, te: (te[i], f, 0))],
        out_specs=pl.BlockSpec((tm,) + SLAB, lambda i, f, te: (i, 0, 0)),
        scratch_shapes=[pltpu.VMEM((tm, D_MODEL), F32), pltpu.VMEM((tm, D_MODEL), BF16)])
    return pl.pallas_call(
        functools.partial(_gmm_kernel, n_tiles=n_tiles),
        grid_spec=grid_spec,
        out_shape=jax.ShapeDtypeStruct((m,) + SLAB, F32),
        compiler_params=_cparams("parallel", "arbitrary"),
        name="grouped_swiglu",
    )(tile_expert, xs, wg, wu, wd)


def _combine_kernel(x_ref, y0_ref, y1_ref, meta_ref, p_ref, pg_ref, pp_ref, lg_ref, lb_ref, *rest):
    o_ref = rest[-1]
    x = x_ref[...]
    meta = meta_ref[...]
    g0 = meta[:, META_G0:META_G0 + 1]
    g1 = meta[:, META_G1:META_G1 + 1]
    ple = (jax.nn.sigmoid(_dot(x.astype(BF16), pg_ref[...]))
           * _dot(p_ref[...].astype(BF16), pp_ref[...]))
    o = DEEPNORM_ALPHA * x + g0 * _from_slabs(y0_ref) + g1 * _from_slabs(y1_ref) + ple
    o_ref[...] = _layer_norm(o, lg_ref[...], lb_ref[...])


def _combine_ln2(x2, gathered, meta, p2, pg, pp, lg, lb, tm, first_block, out_prev):
    n = x2.shape[0]
    ple_dim = p2.shape[1]
    nb = gathered.shape[0] // 2 // tm

    def tok(width):
        return pl.BlockSpec((tm, width), lambda i: (i + first_block, 0))

    def const(shape):
        return pl.BlockSpec(shape, lambda i: (0, 0))

    operands = [x2, gathered, gathered, meta, p2, pg, pp, lg, lb]
    in_specs = [tok(D_MODEL), pl.BlockSpec((tm,) + SLAB, lambda i: (i, 0, 0)),
                pl.BlockSpec((tm,) + SLAB, lambda i: (i + nb, 0, 0)),
                tok(128), tok(ple_dim), const((D_MODEL, D_MODEL)), const((ple_dim, D_MODEL)),
                const((1, D_MODEL)), const((1, D_MODEL))]
    aliases = {}
    if out_prev is not None:
        operands.append(out_prev)
        in_specs.append(pl.BlockSpec(memory_space=pl.ANY))
        aliases = {9: 0}
    return pl.pallas_call(
        _combine_kernel,
        grid=(nb,),
        in_specs=in_specs,
        out_specs=tok(D_MODEL),
        out_shape=jax.ShapeDtypeStruct((n, D_MODEL), F32),
        input_output_aliases=aliases,
        compiler_params=_cparams("parallel"),
        name="combine_ple_ln2",
    )(*operands)


def _moe(x2, w_router, wg, wu, wd, p2, pg, pp, lg, lb, tm, tg, tf):
    n = x2.shape[0]
    wr = jnp.pad(w_router, ((0, 0), (0, 128 - N_EXPERTS)))
    meta, counts, x_slabs = _router(x2, wr, tm)
    cnt = counts[0, :N_EXPERTS].astype(jnp.int32)
    padded = ((cnt + tg - 1) // tg) * tg
    ends = jnp.cumsum(padded)
    starts = ends - padded
    experts = jnp.arange(N_EXPERTS, dtype=jnp.int32)

    def dest(e_lane, rank_lane):
        e = meta[:, e_lane].astype(jnp.int32)
        start = jnp.sum(jnp.where(e[:, None] == experts[None, :], starts[None, :], 0), axis=1)
        return start + meta[:, rank_lane].astype(jnp.int32)

    dest01 = jnp.concatenate([dest(META_E0, META_RANK0), dest(META_E1, META_RANK1)])
    m = 2 * n + N_EXPERTS * tg
    n_tiles = m // tg
    tile_start = jnp.arange(n_tiles, dtype=jnp.int32) * tg
    tile_expert = jnp.minimum(
        jnp.sum((ends[None, :] <= tile_start[:, None]).astype(jnp.int32), axis=1), N_EXPERTS - 1)
    onehot = tile_expert[:, None] == experts[None, :]
    tile_end = jnp.sum(jnp.where(onehot, (starts + cnt)[None, :], 0), axis=1)
    live_rows = jnp.clip(tile_end - tile_start, 0, tg)
    prefetch = jnp.concatenate([tile_expert, (ends[-1] // tg)[None], live_rows]).astype(jnp.int32)
    xs = _scatter_slabs(x_slabs, dest01, m)
    ys = _grouped_swiglu(prefetch, xs, wg, wu, wd, tg, tf)
    out = None
    half = n // 2
    for part in range(2):
        lo = part * half
        rows = jnp.concatenate([dest01[lo:lo + half], dest01[n + lo:n + lo + half]])
        out = _combine_ln2(x2, _gather_slabs(ys, rows), meta, p2, pg, pp, lg, lb, tm,
                           lo // tm, out)
    return out


def _ffn_kernel(x_ref, wg_ref, wu_ref, wd_ref, p_ref, pg_ref, pp_ref, lg_ref, lb_ref,
                o_ref, acc_ref, xb_ref):
    f = pl.program_id(1)

    @pl.when(f == 0)
    def _():
        acc_ref[...] = jnp.zeros_like(acc_ref)
        xb_ref[...] = x_ref[...].astype(BF16)

    xb = xb_ref[...]
    hid = _silu(_dot(xb, wg_ref[...])) * _dot(xb, wu_ref[...])
    acc_ref[...] += _dot(hid.astype(BF16), wd_ref[...])

    @pl.when(f == pl.num_programs(1) - 1)
    def _():
        ple = (jax.nn.sigmoid(_dot(xb, pg_ref[...]))
               * _dot(p_ref[...].astype(BF16), pp_ref[...]))
        o = DEEPNORM_ALPHA * x_ref[...] + acc_ref[...] + ple
        o_ref[...] = _layer_norm(o, lg_ref[...], lb_ref[...])


def _ffn(x2, wg, wu, wd, p2, pg, pp, lg, lb, tm, tf):
    n = x2.shape[0]
    d_ff = wg.shape[1]
    ple_dim = p2.shape[1]

    def tok(width):
        return pl.BlockSpec((tm, width), lambda i, f: (i, 0))

    def const(shape):
        return pl.BlockSpec(shape, lambda i, f: (0, 0))

    return pl.pallas_call(
        _ffn_kernel,
        grid=(n // tm, d_ff // tf),
        in_specs=[tok(D_MODEL),
                  pl.BlockSpec((D_MODEL, tf), lambda i, f: (0, f)),
                  pl.BlockSpec((D_MODEL, tf), lambda i, f: (0, f)),
                  pl.BlockSpec((tf, D_MODEL), lambda i, f: (f, 0)),
                  tok(ple_dim), const((D_MODEL, D_MODEL)), const((ple_dim, D_MODEL)),
                  const((1, D_MODEL)), const((1, D_MODEL))],
        out_specs=tok(D_MODEL),
        out_shape=jax.ShapeDtypeStruct((n, D_MODEL), F32),
        scratch_shapes=[pltpu.VMEM((tm, D_MODEL), F32), pltpu.VMEM((tm, D_MODEL), BF16)],
        compiler_params=_cparams("parallel", "arbitrary"),
        name="ffn_ple_ln2",
    )(x2, wg, wu, wd, p2, pg, pp, lg, lb)


def _prep_w_in(w):
    a, b, c, d, g = jnp.split(w, [768, 1280, 2048, 3336], axis=1)
    z, xs, bc, dt = jnp.split(d, [512, 1024, 1280], axis=1)

    def scale_q(t, scale):
        return jnp.concatenate([t[:, :MIX_W] * scale, t[:, MIX_W:]], axis=1)

    qk_scale = HEAD_DIM ** -0.5
    pad = jnp.zeros((w.shape[0], IN_COLS - (COL_DT + 8)), w.dtype)
    mixers = jnp.concatenate([scale_q(a, qk_scale * LOG2_E), b, scale_q(c, qk_scale), z, xs, bc, dt,
                              pad], axis=1)
    return mixers.astype(BF16), g.astype(BF16)


def _row(v):
    return v.reshape(1, -1).astype(F32)


def _per_head(v):
    return _row(jnp.repeat(v, HEAD_DIM))


def kernel(x, p, w_in, w_br_a, w_br_b, w_br_c, w_br_d, w_out, sg_ln_g, sg_ln_b, sg_w, sg_b,
           ca_rel_bias, ssd_conv_w, ssd_conv_b, ssd_dt_bias, ssd_a_log, ssd_d, ssd_norm_g,
           ln1_g, ln1_b, ffn_w_gate, ffn_w_up, ffn_w_down, moe_router, moe_w_gate, moe_w_up,
           moe_w_down, ple_w_gate, ple_w_proj, ln2_g, ln2_b):
    bsz, seq, _ = x.shape
    n = bsz * seq
    x2 = x.reshape(n, D_MODEL)
    tm_proj = min(1024, n)
    tm = min(512, n)
    sb_blk = min(256, seq)
    ssd_q = min(256, seq)
    sg_tb = min(1024, seq)

    for i in range(DEPTH):
        w_mix, w_gate = _prep_w_in(w_in[i])
        h = _inproj(x2, w_mix, tm_proj, IN_COLS)
        ya = _stick_breaking(h, bsz, seq, sb_blk)
        sg_bias = jnp.repeat(jnp.transpose(sg_b[i]), HEAD_DIM, axis=1).astype(F32)
        yb = _spatial_gating(h, _row(sg_ln_g[i]), _row(sg_ln_b[i]), sg_w[i], sg_bias, sg_tb)
        yc = _band_attention(h, _band_bias(ca_rel_bias[i]), seq)
        cw, cb = ssd_conv_w[i], ssd_conv_b[i]
        yd = _ssd(h, cw[:, :SSD_INNER], _row(cb[:SSD_INNER]), cw[:, SSD_INNER:],
                  _row(cb[SSD_INNER:]), _per_head(ssd_dt_bias[i]), _per_head(ssd_a_log[i]),
                  _per_head(ssd_d[i]), _row(ssd_norm_g[i]), bsz, seq, ssd_q)
        x2 = _merge(x2, w_gate, ya, yb, yc, yd, w_br_a[i].astype(BF16), w_br_b[i].astype(BF16),
                    w_br_c[i].astype(BF16), w_br_d[i].astype(BF16), w_out[i].astype(BF16),
                    _row(ln1_g[i]), _row(ln1_b[i]), tm_proj)
        p2 = p[i].reshape(n, -1)
        pg, pp = ple_w_gate[i].astype(BF16), ple_w_proj[i].astype(BF16)
        j = i // 2
        if i % 2 == 0:
            x2 = _ffn(x2, ffn_w_gate[j].astype(BF16), ffn_w_up[j].astype(BF16),
                      ffn_w_down[j].astype(BF16), p2, pg, pp, _row(ln2_g[i]), _row(ln2_b[i]),
                      tm, 2816)
        else:
            x2 = _moe(x2, moe_router[j], moe_w_gate[j].astype(BF16), moe_w_up[j].astype(BF16),
                      moe_w_down[j].astype(BF16), p2, pg, pp, _row(ln2_g[i]), _row(ln2_b[i]),
                      tm, min(1024, n), 1792)
    return x2.reshape(bsz, seq, D_MODEL)
```
